```python
import math
import jax, jax.numpy as jnp
from jax import lax
import numpy as np


D_MODEL = 1024
BATCH = 2
SEQ = 8192
DEPTH = 4
DEC_BATCH = 128
DEC_SEQ = 8
PAST_LEN = 8192
PAGE_SIZE = 128

N_MLA = (DEPTH + 1) // 2
N_DIFF = DEPTH // 2
MLA_HEADS = 8
Q_LORA = 384
KV_LORA = 256
NOPE = 128
ROPE = 64
V_DIM = 128
ROPE_THETA = 10000.0
MLA_SCALE = (NOPE + ROPE) ** -0.5
DH = 64
DIFF_HEADS = D_MODEL // (2 * DH)
DIFF_KV_HEADS = 2
DIFF_GROUP = DIFF_HEADS // DIFF_KV_HEADS
DIFF_SCALE = DH ** -0.5
DQ = DIFF_HEADS * 2 * DH
DK = DIFF_KV_HEADS * 2 * DH
DV = DIFF_KV_HEADS * 2 * DH
N_BUCKETS = 32
MAX_DISTANCE = 128
FFN_HIDDEN = ((8 * D_MODEL + 3 * 256 - 1) // (3 * 256)) * 256
BLK_Q = 128
EPS = 1e-6

kernel_name = "mla_diffattn_hybrid_step"


def rmsnorm(x, g):
    xf = x.astype(jnp.float32)
    y = xf * lax.rsqrt(jnp.mean(xf * xf, axis=-1, keepdims=True) + EPS)
    return (y * g.astype(jnp.float32)).astype(x.dtype)


def rope(x, pos):
    half = ROPE // 2
    inv_freq = ROPE_THETA ** (-jnp.arange(half, dtype=jnp.float32) / half)
    ang = pos.astype(jnp.float32)[:, None, None] * inv_freq
    cos, sin = jnp.cos(ang), jnp.sin(ang)
    xf = x.astype(jnp.float32)
    x1, x2 = xf[..., :half], xf[..., half:]
    return jnp.concatenate([x1 * cos - x2 * sin, x2 * cos + x1 * sin], axis=-1).astype(x.dtype)


def t5_bias(table, q_pos, k_pos):
    n = jnp.maximum(q_pos[:, None] - k_pos[None, :], 0)
    max_exact = N_BUCKETS // 2
    nf = jnp.maximum(n, 1).astype(jnp.float32)
    large = max_exact + (jnp.log(nf / max_exact) / math.log(MAX_DISTANCE / max_exact)
                         * (N_BUCKETS - max_exact)).astype(jnp.int32)
    large = jnp.minimum(large, N_BUCKETS - 1)
    bucket = jnp.where(n < max_exact, n, large)
    return jnp.moveaxis(table[bucket].astype(jnp.float32), -1, 0)


def sweep_query_blocks(fn, q_args):
    b, s = q_args[0].shape[:2]
    nb = s // BLK_Q
    blocks = tuple(jnp.swapaxes(a.reshape(b, nb, BLK_Q, *a.shape[2:]), 0, 1) for a in q_args)
    pos = jnp.arange(s, dtype=jnp.int32).reshape(nb, BLK_Q)
    out = lax.map(lambda args: fn(*args), blocks + (pos,))
    out = jnp.swapaxes(out, 0, 1)
    return out.reshape(b, s, *out.shape[3:])


def mla_project(h, pos, p):
    d = h @ p["w_in"]
    cq = rmsnorm(d[..., :Q_LORA], p["g_qa"])
    ckv = rmsnorm(d[..., Q_LORA:Q_LORA + KV_LORA], p["g_kva"])
    kr = rope(rmsnorm(d[..., Q_LORA + KV_LORA:], p["g_kr"])[..., None, :], pos)[..., 0, :]
    q = (cq @ p["w_qb"]).reshape(*cq.shape[:-1], MLA_HEADS, NOPE + ROPE)
    q_nope = rmsnorm(q[..., :NOPE], p["g_qn"])
    q_rope = rope(rmsnorm(q[..., NOPE:], p["g_qr"]), pos)
    return q_nope, q_rope, ckv, kr


def mla_expand(ckv, p):
    kv = (ckv @ p["w_kvb"]).reshape(*ckv.shape[:-1], MLA_HEADS, NOPE + V_DIM)
    return rmsnorm(kv[..., :NOPE], p["g_kn"]), kv[..., NOPE:]


def mla_attend(q_nope, q_rope, k_nope, k_rope, v, q_pos, k_pos):
    s = jnp.einsum('qhd,khd->hqk', q_nope, k_nope) + jnp.einsum('qhr,kr->hqk', q_rope, k_rope)
    s = s.astype(jnp.float32) * MLA_SCALE
    s = jnp.where(k_pos[None, None, :] <= q_pos[None, :, None], s, -jnp.inf)
    prob = jax.nn.softmax(s, axis=-1)
    return jnp.einsum('hqk,khv->qhv', prob.astype(v.dtype), v)


def mla_prompt(h, p):
    b, s = h.shape[:2]
    pos = jnp.arange(s, dtype=jnp.int32)
    q_nope, q_rope, ckv, kr = mla_project(h, pos, p)
    k_nope, v = mla_expand(ckv, p)

    def block(qn, qr, qp):
        return jax.vmap(lambda a, c, kn, kr_, vv: mla_attend(a, c, kn, kr_, vv, qp, pos))(qn, qr, k_nope, kr, v)

    o = sweep_query_blocks(block, (q_nope, q_rope))
    return o.reshape(b, s, MLA_HEADS * V_DIM) @ p["w_o"], ckv, kr


def mla_sample(h, cache_ckv_l, cache_kr_l, page_table, p):
    b, t = h.shape[:2]
    past = page_table.shape[1] * PAGE_SIZE
    pos = past + jnp.arange(t, dtype=jnp.int32)
    k_pos = jnp.arange(past + t, dtype=jnp.int32)
    q_nope, q_rope, ckv, kr = mla_project(h, pos, p)

    def one(args):
        pt, qn, qr, c_new, r_new = args
        c = jnp.concatenate([cache_ckv_l[pt].reshape(past, KV_LORA), c_new], axis=0)
        r = jnp.concatenate([cache_kr_l[pt].reshape(past, ROPE), r_new], axis=0)
        kn, v = mla_expand(c, p)
        return mla_attend(qn, qr, kn, r, v, pos, k_pos)

    o = lax.map(one, (page_table, q_nope, q_rope, ckv, kr))
    return o.reshape(b, t, MLA_HEADS * V_DIM) @ p["w_o"], ckv, kr


def diff_project(h, p):
    qkv = h @ p["w_qkv"]
    lead = qkv.shape[:-1]
    q = rmsnorm(qkv[..., :DQ].reshape(*lead, DIFF_HEADS, 2, DH), p["g_qn"])
    k = rmsnorm(qkv[..., DQ:DQ + DK].reshape(*lead, DIFF_KV_HEADS, 2, DH), p["g_kn"])
    v = qkv[..., DQ + DK:].reshape(*lead, DIFF_KV_HEADS, 2 * DH)
    return q, k.reshape(*lead, DIFF_KV_HEADS, 2 * DH), v


def diff_attend(q, k, v, table, lam, q_pos, k_pos):
    tq, tk = q.shape[0], k.shape[0]
    qg = q.reshape(tq, DIFF_KV_HEADS, DIFF_GROUP, 2, DH)
    kg = k.reshape(tk, DIFF_KV_HEADS, 2, DH)
    s = jnp.einsum('qngmd,tnmd->mngqt', qg, kg).astype(jnp.float32) * DIFF_SCALE
    bias = t5_bias(table, q_pos, k_pos).reshape(DIFF_KV_HEADS, DIFF_GROUP, tq, tk)
    s = jnp.where(k_pos[None, None, None, None, :] <= q_pos[None, None, None, :, None], s + bias[None], -jnp.inf)
    prob = jax.nn.softmax(s, axis=-1)
    a = prob[0] - lam * prob[1]
    o = jnp.einsum('ngqt,tnv->qngv', a.astype(v.dtype), v)
    return o.reshape(tq, DIFF_HEADS, 2 * DH)


def diff_out(o, p, lam_init):
    b, t = o.shape[:2]
    o = rmsnorm(o, p["g_sub"]) * (1.0 - lam_init)
    return o.reshape(b, t, DIFF_HEADS * 2 * DH) @ p["w_o"]


def diff_prompt(h, table, p, lam, lam_init):
    s = h.shape[1]
    pos = jnp.arange(s, dtype=jnp.int32)
    q, k, v = diff_project(h, p)

    def block(qb, qp):
        return jax.vmap(lambda a, kk, vv: diff_attend(a, kk, vv, table, lam, qp, pos))(qb, k, v)

    o = sweep_query_blocks(block, (q,))
    return diff_out(o, p, lam_init), k, v


def diff_sample(h, cache_k_l, cache_v_l, page_table, table, p, lam, lam_init):
    t = h.shape[1]
    past = page_table.shape[1] * PAGE_SIZE
    pos = past + jnp.arange(t, dtype=jnp.int32)
    k_pos = jnp.arange(past + t, dtype=jnp.int32)
    q, k_new, v_new = diff_project(h, p)

    def one(args):
        pt, qq, kn, vn = args
        kk = jnp.concatenate([cache_k_l[pt].reshape(past, DIFF_KV_HEADS, 2 * DH), kn], axis=0)
        vv = jnp.concatenate([cache_v_l[pt].reshape(past, DIFF_KV_HEADS, 2 * DH), vn], axis=0)
        return diff_attend(qq, kk, vv, table, lam, pos, k_pos)

    o = lax.map(one, (page_table, q, k_new, v_new))
    return diff_out(o, p, lam_init), k_new, v_new


def swiglu(h, w_in, w_out):
    gu = h @ w_in
    return (jax.nn.silu(gu[..., :FFN_HIDDEN]) * gu[..., FFN_HIDDEN:]) @ w_out


def to_pages(a):
    return a.reshape(a.shape[0], a.shape[1] // PAGE_SIZE, PAGE_SIZE, *a.shape[2:])


def setup_inputs(seed: int = 0) -> dict:
    key = jax.random.key(seed)
    ks = jax.random.split(key, 40)
    f32 = jnp.float32

    def nrm(k, shape, fan_in):
        return jax.random.normal(k, shape, f32) * (fan_in ** -0.5)

    def gain(k, shape):
        return 1.0 + 0.1 * jax.random.normal(k, shape, f32)

    n_pages = PAST_LEN // PAGE_SIZE
    n_pool = (5 * DEC_BATCH * n_pages) // 4
    page_table = jax.random.permutation(ks[0], n_pool)[:DEC_BATCH * n_pages].reshape(DEC_BATCH, n_pages).astype(jnp.int32)
    return {
        "x_prompt": jax.random.normal(ks[1], (BATCH, SEQ, D_MODEL), f32),
        "x_sample": jax.random.normal(ks[2], (DEC_BATCH, DEC_SEQ, D_MODEL), f32),
        "cache_mla_ckv": jax.random.normal(ks[3], (N_MLA, n_pool, PAGE_SIZE, KV_LORA), f32),
        "cache_mla_krope": jax.random.normal(ks[4], (N_MLA, n_pool, PAGE_SIZE, ROPE), f32),
        "cache_diff_k": jax.random.normal(ks[5], (N_DIFF, n_pool, PAGE_SIZE, DIFF_KV_HEADS, 2 * DH), f32),
        "cache_diff_v": jax.random.normal(ks[6], (N_DIFF, n_pool, PAGE_SIZE, DIFF_KV_HEADS, 2 * DH), f32),
        "page_table": page_table,
        "rel_bias": 0.5 * jax.random.normal(ks[7], (N_BUCKETS, DIFF_HEADS), f32),
        "g_mix_norm": gain(ks[8], (DEPTH, D_MODEL)),
        "g_ffn_norm": gain(ks[9], (DEPTH, D_MODEL)),
        "mla_w_in": nrm(ks[10], (N_MLA, D_MODEL, Q_LORA + KV_LORA + ROPE), D_MODEL),
        "mla_g_q_a": gain(ks[11], (N_MLA, Q_LORA)),
        "mla_g_kv_a": gain(ks[12], (N_MLA, KV_LORA)),
        "mla_w_q_b": nrm(ks[13], (N_MLA, Q_LORA, MLA_HEADS * (NOPE + ROPE)), Q_LORA),
        "mla_w_kv_b": nrm(ks[14], (N_MLA, KV_LORA, MLA_HEADS * (NOPE + V_DIM)), KV_LORA),
        "mla_g_qn": gain(ks[15], (N_MLA, NOPE)),
        "mla_g_qr": gain(ks[16], (N_MLA, ROPE)),
        "mla_g_kn": gain(ks[17], (N_MLA, NOPE)),
        "mla_g_kr": gain(ks[18], (N_MLA, ROPE)),
        "mla_w_o": nrm(ks[19], (N_MLA, MLA_HEADS * V_DIM, D_MODEL), MLA_HEADS * V_DIM),
        "diff_w_qkv": nrm(ks[20], (N_DIFF, D_MODEL, DQ + DK + DV), D_MODEL),
        "diff_g_qn": gain(ks[21], (N_DIFF, DH)),
        "diff_g_kn": gain(ks[22], (N_DIFF, DH)),
        "diff_lq1": 0.1 * jax.random.normal(ks[23], (N_DIFF, DH), f32),
        "diff_lk1": 0.1 * jax.random.normal(ks[24], (N_DIFF, DH), f32),
        "diff_lq2": 0.1 * jax.random.normal(ks[25], (N_DIFF, DH), f32),
        "diff_lk2": 0.1 * jax.random.normal(ks[26], (N_DIFF, DH), f32),
        "diff_g_sub": gain(ks[27], (N_DIFF, 2 * DH)),
        "diff_w_o": nrm(ks[28], (N_DIFF, DIFF_HEADS * 2 * DH, D_MODEL), DIFF_HEADS * 2 * DH),
        "ffn_w_in": nrm(ks[29], (DEPTH, D_MODEL, 2 * FFN_HIDDEN), D_MODEL),
        "ffn_w_out": nrm(ks[30], (DEPTH, FFN_HIDDEN, D_MODEL), FFN_HIDDEN),
    }


def reference(x_prompt, x_sample, cache_mla_ckv, cache_mla_krope, cache_diff_k, cache_diff_v, page_table,
              rel_bias, g_mix_norm, g_ffn_norm,
              mla_w_in, mla_g_q_a, mla_g_kv_a, mla_w_q_b, mla_w_kv_b, mla_g_qn, mla_g_qr, mla_g_kn, mla_g_kr, mla_w_o,
              diff_w_qkv, diff_g_qn, diff_g_kn, diff_lq1, diff_lk1, diff_lq2, diff_lk2, diff_g_sub, diff_w_o,
              ffn_w_in, ffn_w_out):
    xp, xs = x_prompt, x_sample
    mla_ckv_p, mla_kr_p, mla_ckv_s, mla_kr_s = [], [], [], []
    diff_k_p, diff_v_p, diff_k_s, diff_v_s = [], [], [], []
    for i in range(DEPTH):
        j = i // 2
        hp = rmsnorm(xp, g_mix_norm[i])
        hs = rmsnorm(xs, g_mix_norm[i])
        if i % 2 == 0:
            p = {"w_in": mla_w_in[j], "g_qa": mla_g_q_a[j], "g_kva": mla_g_kv_a[j], "w_qb": mla_w_q_b[j],
                 "w_kvb": mla_w_kv_b[j], "g_qn": mla_g_qn[j], "g_qr": mla_g_qr[j], "g_kn": mla_g_kn[j],
                 "g_kr": mla_g_kr[j], "w_o": mla_w_o[j]}
            op, c_p, r_p = mla_prompt(hp, p)
            os_, c_s, r_s = mla_sample(hs, cache_mla_ckv[j], cache_mla_krope[j], page_table, p)
            mla_ckv_p.append(to_pages(c_p)); mla_kr_p.append(to_pages(r_p))
            mla_ckv_s.append(c_s); mla_kr_s.append(r_s)
        else:
            p = {"w_qkv": diff_w_qkv[j], "g_qn": diff_g_qn[j], "g_kn": diff_g_kn[j],
                 "g_sub": diff_g_sub[j], "w_o": diff_w_o[j]}
            lam_init = 0.8 - 0.6 * math.exp(-0.3 * i)
            lam = (jnp.exp(jnp.sum(diff_lq1[j].astype(jnp.float32) * diff_lk1[j].astype(jnp.float32)))
                   - jnp.exp(jnp.sum(diff_lq2[j].astype(jnp.float32) * diff_lk2[j].astype(jnp.float32)))
                   + lam_init)
            op, k_p, v_p = diff_prompt(hp, rel_bias, p, lam, lam_init)
            os_, k_s, v_s = diff_sample(hs, cache_diff_k[j], cache_diff_v[j], page_table, rel_bias, p, lam, lam_init)
            diff_k_p.append(to_pages(k_p)); diff_v_p.append(to_pages(v_p))
            diff_k_s.append(k_s); diff_v_s.append(v_s)
        xp = xp + op
        xs = xs + os_
        xp = xp + swiglu(rmsnorm(xp, g_ffn_norm[i]), ffn_w_in[i], ffn_w_out[i])
        xs = xs + swiglu(rmsnorm(xs, g_ffn_norm[i]), ffn_w_in[i], ffn_w_out[i])
    return (xp, xs,
            jnp.stack(mla_ckv_p), jnp.stack(mla_kr_p), jnp.stack(mla_ckv_s), jnp.stack(mla_kr_s),
            jnp.stack(diff_k_p), jnp.stack(diff_v_p), jnp.stack(diff_k_s), jnp.stack(diff_v_s))
```

```python
import functools
import math

import jax
import jax.numpy as jnp
from jax import lax
from jax.experimental import pallas as pl
from jax.experimental.pallas import tpu as pltpu

F32 = jnp.float32
BF16 = jnp.bfloat16

EPS = 1e-6
PAGE_SIZE = 128
MLA_HEADS = 8
Q_LORA = 384
KV_LORA = 256
NOPE = 128
ROPE = 64
V_DIM = 128
ROPE_THETA = 10000.0
MLA_SCALE = (NOPE + ROPE) ** -0.5
DH = 64
DIFF_KV_HEADS = 2
DIFF_SCALE = DH ** -0.5
N_BUCKETS = 32
MAX_DISTANCE = 128

LANES = 128
VMEM_LIMIT = 56 * 1024 * 1024

NT_DIMS = (((1,), (1,)), ((), ()))


def _cparams(sem):
    return pltpu.CompilerParams(dimension_semantics=sem, vmem_limit_bytes=VMEM_LIMIT)


def _rms(x, g):
    return x * lax.rsqrt(jnp.mean(x * x, axis=-1, keepdims=True) + EPS) * g


def _dot(a, b):
    return jnp.dot(a, b, preferred_element_type=F32)


def _dot_nt(a, b):
    return lax.dot_general(a, b, NT_DIMS, preferred_element_type=F32)


def _pad_rows(a, rows):
    return jnp.concatenate([a, jnp.zeros((rows - a.shape[0], a.shape[1]), a.dtype)], axis=0)


def _full(shape):
    return pl.BlockSpec(shape, lambda *_: (0,) * len(shape))


def _row_tile(n, cap):
    t = min(cap, n)
    assert n % t == 0 and t % 8 == 0
    return t


def _mla_proj_kernel(x_ref, cos_ref, sin_ref, gmix_ref, win_ref, gqa_ref, gkva_ref, gkr_ref, gkrr_ref,
                     wqb_ref, gqn_ref, gqr_ref, gqrr_ref, gkn_ref, wk_ref, *rest, prompt):
    if prompt:
        wv_ref, ckv_ref, kr_ref, q_ref, k_ref, v_ref = rest
    else:
        ckv_ref, kr_ref, qa_ref, qr_ref = rest
    cos2, sin2 = cos_ref[...], sin_ref[...]
    h = _rms(x_ref[...], gmix_ref[...]).astype(BF16)
    d = _dot(h, win_ref[...])
    cq = _rms(d[:, :Q_LORA], gqa_ref[...]).astype(BF16)
    ckv = _rms(d[:, Q_LORA:Q_LORA + KV_LORA], gkva_ref[...])
    ckv_ref[...] = ckv
    zr = d[:, Q_LORA + KV_LORA:Q_LORA + KV_LORA + LANES]
    zrr = d[:, Q_LORA + KV_LORA + LANES:]
    inv = lax.rsqrt(jnp.sum(zr * zr, axis=-1, keepdims=True) * (1.0 / ROPE) + EPS)
    kr = (zr * gkr_ref[...] * cos2 + zrr * gkrr_ref[...] * sin2) * inv
    kr_ref[...] = kr[:, :ROPE]

    qall = _dot(cq, wqb_ref[...])
    hw = MLA_HEADS * LANES
    cb = ckv.astype(BF16)
    if prompt:
        kn_all = _dot(cb, wk_ref[...])
        v_ref[...] = _dot(cb, wv_ref[...]).astype(BF16)
        krb = kr.astype(BF16)
    for hd in range(MLA_HEADS):
        c0 = hd * LANES
        qn = _rms(qall[:, c0:c0 + LANES], gqn_ref[...])
        zr = qall[:, hw + c0:hw + c0 + LANES]
        zrr = qall[:, 2 * hw + c0:2 * hw + c0 + LANES]
        inv = lax.rsqrt(jnp.sum(zr * zr, axis=-1, keepdims=True) * (1.0 / ROPE) + EPS)
        qr = ((zr * gqr_ref[...] * cos2 + zrr * gqrr_ref[...] * sin2) * inv).astype(BF16)
        if prompt:
            q_ref[:, 2 * c0:2 * c0 + LANES] = qn.astype(BF16)
            q_ref[:, 2 * c0 + LANES:2 * c0 + 2 * LANES] = qr
            kn = _rms(kn_all[:, c0:c0 + LANES], gkn_ref[...])
            k_ref[:, 2 * c0:2 * c0 + LANES] = kn.astype(BF16)
            k_ref[:, 2 * c0 + LANES:2 * c0 + 2 * LANES] = krb
        else:
            qg = (qn * gkn_ref[...]).astype(BF16)
            qa_ref[:, hd * KV_LORA:(hd + 1) * KV_LORA] = _dot(qg, wk_ref[c0:c0 + LANES, :])
            qr_ref[:, c0:c0 + LANES] = qr.astype(F32)


def _mla_proj(x, cos2, sin2, p, prompt):
    n, dm = x.shape
    tm = _row_tile(n, 512)
    nt = cos2.shape[0] // tm
    hw = MLA_HEADS * LANES
    row = lambda w: pl.BlockSpec((tm, w), lambda i: (i, 0))
    tab = pl.BlockSpec((tm, LANES), lambda i: (i % nt, 0))
    ins = [x, cos2, sin2, p["gmix"], p["w_in"], p["g_qa"], p["g_kva"], p["g_kr"], p["g_kr_rot"],
           p["w_qb"], p["g_qn"], p["g_qr"], p["g_qr_rot"], p["g_kn"]]
    specs = [row(dm), tab, tab] + [_full(a.shape) for a in ins[3:]]
    if prompt:
        ins += [p["w_k"], p["w_v"]]
        out_shape = [jax.ShapeDtypeStruct((n, KV_LORA), F32), jax.ShapeDtypeStruct((n, ROPE), F32),
                     jax.ShapeDtypeStruct((n, 2 * hw), BF16), jax.ShapeDtypeStruct((n, 2 * hw), BF16),
                     jax.ShapeDtypeStruct((n, hw), BF16)]
        out_specs = [row(KV_LORA), row(ROPE), row(2 * hw), row(2 * hw), row(hw)]
    else:
        ins += [p["w_k_t"]]
        out_shape = [jax.ShapeDtypeStruct((n, KV_LORA), F32), jax.ShapeDtypeStruct((n, ROPE), F32),
                     jax.ShapeDtypeStruct((n, MLA_HEADS * KV_LORA), F32), jax.ShapeDtypeStruct((n, hw), F32)]
        out_specs = [row(KV_LORA), row(ROPE), row(MLA_HEADS * KV_LORA), row(hw)]
    specs += [_full(a.shape) for a in ins[len(specs):]]
    return pl.pallas_call(
        functools.partial(_mla_proj_kernel, prompt=prompt),
        grid=(n // tm,), in_specs=specs, out_specs=out_specs, out_shape=out_shape,
        compiler_params=_cparams(("parallel",)),
        name="mla_proj_prompt" if prompt else "mla_proj_sample",
    )(*ins)


def _softmax_step(s, v, m_ref, l_ref, acc_ref):
    m_old = m_ref[...]
    m_new = jnp.maximum(m_old, jnp.max(s, axis=-1, keepdims=True))
    alpha = jnp.exp(m_old - m_new)
    p = jnp.exp(s - m_new)
    l_ref[...] = alpha * l_ref[...] + jnp.sum(p, axis=-1, keepdims=True)
    acc_ref[...] = alpha * acc_ref[...] + _dot(p.astype(BF16), v)
    m_ref[...] = m_new


def _mla_attn_kernel(q_ref, k_ref, v_ref, o_ref, m_ref, l_ref, acc_ref, *, t):
    i = pl.program_id(2)
    m_ref[...] = jnp.full(m_ref.shape, -jnp.inf, F32)
    l_ref[...] = jnp.zeros(l_ref.shape, F32)
    acc_ref[...] = jnp.zeros(acc_ref.shape, F32)
    q = q_ref[...]

    def scores(j):
        off = pl.multiple_of(j * t, t)
        return _dot_nt(q, k_ref[pl.ds(off, t), :]), v_ref[pl.ds(off, t), :]

    def body(j, carry):
        s, v = scores(j)
        _softmax_step(s, v, m_ref, l_ref, acc_ref)
        return carry

    lax.fori_loop(0, i, body, 0)
    s, v = scores(i)
    rows = lax.broadcasted_iota(jnp.int32, (t, t), 0)
    cols = lax.broadcasted_iota(jnp.int32, (t, t), 1)
    _softmax_step(jnp.where(cols <= rows, s, -jnp.inf), v, m_ref, l_ref, acc_ref)
    o_ref[...] = (acc_ref[...] / l_ref[...]).astype(BF16)


def _mla_prompt_attn(q, k, v, batch, seq):
    t = _row_tile(seq, 512)
    nq = seq // t
    kw = 2 * LANES
    return pl.pallas_call(
        functools.partial(_mla_attn_kernel, t=t),
        grid=(batch, MLA_HEADS, nq),
        in_specs=[pl.BlockSpec((t, kw), lambda b, h, i: (b * nq + i, h)),
                  pl.BlockSpec((seq, kw), lambda b, h, i: (b, h)),
                  pl.BlockSpec((seq, V_DIM), lambda b, h, i: (b, h))],
        out_specs=pl.BlockSpec((t, V_DIM), lambda b, h, i: (b * nq + i, h)),
        out_shape=jax.ShapeDtypeStruct((batch * seq, MLA_HEADS * V_DIM), BF16),
        scratch_shapes=[pltpu.VMEM((t, 1), F32), pltpu.VMEM((t, 1), F32), pltpu.VMEM((t, V_DIM), F32)],
        compiler_params=_cparams(("parallel", "parallel", "arbitrary")),
        name="mla_prompt_attn",
    )(q, k, v)


def _mla_sample_kernel(pt_ref, qa_ref, qr_ref, *rest, pages, chunk):
    c_pages = rest[:pages]
    r_pages = rest[pages:2 * pages]
    (cnew_ref, rnew_ref, wkt_ref, wv_ref, o_ref,
     qs_ref, qrs_ref, c_ref, r_ref, m_ref, l_ref, u_ref) = rest[2 * pages:]
    j = pl.program_id(1)
    t_new = cnew_ref.shape[0]
    rows = MLA_HEADS * t_new

    @pl.when(j == 0)
    def _():
        m_ref[...] = jnp.full(m_ref.shape, -jnp.inf, F32)
        l_ref[...] = jnp.zeros(l_ref.shape, F32)
        u_ref[...] = jnp.zeros(u_ref.shape, F32)
        qs_ref[...] = jnp.concatenate(
            [qa_ref[:, hd * KV_LORA:(hd + 1) * KV_LORA] for hd in range(MLA_HEADS)], axis=0).astype(BF16)
        qrs_ref[...] = jnp.concatenate(
            [qr_ref[:, hd * LANES:hd * LANES + ROPE] for hd in range(MLA_HEADS)], axis=0).astype(BF16)

    def attend(c, r, mask):
        n = c.shape[0]
        kt = _dot_nt(wkt_ref[...], c)
        ssq = jnp.sum((kt * kt).reshape(MLA_HEADS, NOPE, n), axis=1)
        inv = lax.rsqrt(ssq * (1.0 / NOPE) + EPS)
        s = _dot_nt(qs_ref[...], c).reshape(MLA_HEADS, t_new, n) * inv[:, None, :]
        s = s.reshape(rows, n) + _dot_nt(qrs_ref[...], r)
        if mask is not None:
            s = jnp.where(mask, s, -jnp.inf)
        _softmax_step(s, c, m_ref, l_ref, u_ref)

    for pg in range(pages):
        c_ref[pg * PAGE_SIZE:(pg + 1) * PAGE_SIZE, :] = c_pages[pg][...].astype(BF16)
        r_ref[pg * PAGE_SIZE:(pg + 1) * PAGE_SIZE, :] = r_pages[pg][...].astype(BF16)
    for c0 in range(0, pages * PAGE_SIZE, chunk):
        attend(c_ref[c0:c0 + chunk, :], r_ref[c0:c0 + chunk, :], None)

    @pl.when(j == pl.num_programs(1) - 1)
    def _():
        c_ref[0:LANES, :] = _pad_rows(cnew_ref[...], LANES).astype(BF16)
        r_ref[0:LANES, :] = _pad_rows(rnew_ref[...], LANES).astype(BF16)
        tq = lax.broadcasted_iota(jnp.int32, (rows, LANES), 0) % t_new
        tk = lax.broadcasted_iota(jnp.int32, (rows, LANES), 1)
        attend(c_ref[0:LANES, :], r_ref[0:LANES, :], tk <= tq)
        o_all = _dot((u_ref[...] / l_ref[...]).astype(BF16), wv_ref[...])
        for hd in range(MLA_HEADS):
            o_ref[:, hd * V_DIM:(hd + 1) * V_DIM] = o_all[hd * t_new:(hd + 1) * t_new, hd * V_DIM:(hd + 1) * V_DIM]


def _mla_sample_attn(qa, qr, cache_c, cache_r, layer, c_new, r_new, page_table, p):
    b, t_new, _ = qa.shape
    n_pages = page_table.shape[1]
    pages = min(8, n_pages)
    assert n_pages % pages == 0
    steps = n_pages // pages
    chunk = min(512, pages * PAGE_SIZE)
    rows = MLA_HEADS * t_new
    per_seq = lambda w: pl.BlockSpec((None, t_new, w), lambda i, j, pt: (i, 0, 0))

    def page_spec(w, pg):
        return pl.BlockSpec((None, None, PAGE_SIZE, w),
                            lambda i, j, pt: (layer, pt[i * n_pages + j * pages + pg], 0, 0))

    in_specs = ([per_seq(MLA_HEADS * KV_LORA), per_seq(MLA_HEADS * LANES)]
                + [page_spec(KV_LORA, pg) for pg in range(pages)]
                + [page_spec(ROPE, pg) for pg in range(pages)]
                + [per_seq(KV_LORA), per_seq(ROPE),
                   pl.BlockSpec(p["w_k_t"].shape, lambda i, j, pt: (0, 0)),
                   pl.BlockSpec(p["w_v"].shape, lambda i, j, pt: (0, 0))])
    grid_spec = pltpu.PrefetchScalarGridSpec(
        num_scalar_prefetch=1, grid=(b, steps), in_specs=in_specs,
        out_specs=per_seq(MLA_HEADS * V_DIM),
        scratch_shapes=[pltpu.VMEM((rows, KV_LORA), BF16), pltpu.VMEM((rows, ROPE), BF16),
                        pltpu.VMEM((pages * PAGE_SIZE, KV_LORA), BF16), pltpu.VMEM((pages * PAGE_SIZE, ROPE), BF16),
                        pltpu.VMEM((rows, 1), F32), pltpu.VMEM((rows, 1), F32), pltpu.VMEM((rows, KV_LORA), F32)])
    return pl.pallas_call(
        functools.partial(_mla_sample_kernel, pages=pages, chunk=chunk),
        grid_spec=grid_spec,
        out_shape=jax.ShapeDtypeStruct((b, t_new, MLA_HEADS * V_DIM), F32),
        compiler_params=_cparams(("parallel", "arbitrary")),
        name="mla_sample_attn",
    )(page_table.reshape(-1), qa, qr, *([cache_c] * pages), *([cache_r] * pages), c_new, r_new,
      p["w_k_t"], p["w_v"])


def _diff_proj_kernel(x_ref, gmix_ref, w_ref, gq_ref, gk_ref, ones_ref, q_ref, k_ref, v_ref, kb_ref, vb_ref, *, dq):
    h = _rms(x_ref[...], gmix_ref[...]).astype(BF16)
    qkv = _dot(h, w_ref[...])
    dk = k_ref.shape[1]

    def norm64(z, g):
        ms = _dot((z * z).astype(BF16), ones_ref[...]) * (1.0 / DH)
        return z * lax.rsqrt(ms + EPS) * g

    for c0 in range(0, dq, LANES):
        q_ref[:, c0:c0 + LANES] = norm64(qkv[:, c0:c0 + LANES], gq_ref[...]).astype(BF16)
    for c0 in range(0, dk, LANES):
        k = norm64(qkv[:, dq + c0:dq + c0 + LANES], gk_ref[...])
        k_ref[:, c0:c0 + LANES] = k
        kb_ref[:, c0:c0 + LANES] = k.astype(BF16)
    v = qkv[:, dq + dk:]
    v_ref[...] = v
    vb_ref[...] = v.astype(BF16)


def _diff_proj(x, p):
    n, dm = x.shape
    tm = _row_tile(n, 512)
    dq = dm
    dk = DIFF_KV_HEADS * 2 * DH
    row = lambda w: pl.BlockSpec((tm, w), lambda i: (i, 0))
    ins = [x, p["gmix"], p["w_qkv"], p["g_q"], p["g_k"], p["ones64"]]
    return pl.pallas_call(
        functools.partial(_diff_proj_kernel, dq=dq),
        grid=(n // tm,),
        in_specs=[row(dm)] + [_full(a.shape) for a in ins[1:]],
        out_specs=[row(dq), row(dk), row(dk), row(dk), row(dk)],
        out_shape=[jax.ShapeDtypeStruct((n, dq), BF16), jax.ShapeDtypeStruct((n, dk), F32),
                   jax.ShapeDtypeStruct((n, dk), F32), jax.ShapeDtypeStruct((n, dk), BF16),
                   jax.ShapeDtypeStruct((n, dk), BF16)],
        compiler_params=_cparams(("parallel",)),
        name="diff_proj",
    )(*ins)


def _stack_diff_queries(q, qs_ref, t, col0):
    lane = lax.broadcasted_iota(jnp.int32, (t, LANES), 1)
    zero = jnp.zeros((t, LANES), q.dtype)
    for g in range(q.shape[1] // LANES):
        qg = q[:, g * LANES:(g + 1) * LANES]
        qs_ref[(2 * g) * t:(2 * g + 1) * t, col0:col0 + LANES] = jnp.where(lane < DH, qg, zero).astype(qs_ref.dtype)
        qs_ref[(2 * g + 1) * t:(2 * g + 2) * t, col0:col0 + LANES] = jnp.where(lane >= DH, qg, zero).astype(qs_ref.dtype)


def _diff_attn_kernel(lam_ref, q_ref, k_ref, v_ref, bias_ref, gsub_ref, o_ref, qs_ref, m_ref, l_ref, acc_ref, *, t):
    i = pl.program_id(2)
    group = q_ref.shape[1] // LANES
    m_ref[...] = jnp.full(m_ref.shape, -jnp.inf, F32)
    l_ref[...] = jnp.zeros(l_ref.shape, F32)
    acc_ref[...] = jnp.zeros(acc_ref.shape, F32)
    _stack_diff_queries(q_ref[...], qs_ref, t, 0)

    def step(j, near):
        off = pl.multiple_of(j * t, t)
        s = _dot_nt(qs_ref[...], k_ref[pl.ds(off, t), :])
        if near is not None:
            s = (s.reshape(group, 2, t, t) + bias_ref[near][:, None]).reshape(group * 2 * t, t)
        if near == 0:
            rows = lax.broadcasted_iota(jnp.int32, s.shape, 0) % t
            cols = lax.broadcasted_iota(jnp.int32, s.shape, 1)
            s = jnp.where(cols <= rows, s, -jnp.inf)
        _softmax_step(s, v_ref[pl.ds(off, t), :], m_ref, l_ref, acc_ref)

    def far(j, carry):
        step(j, None)
        return carry

    lax.fori_loop(0, i - 1, far, 0)

    @pl.when(i >= 1)
    def _():
        step(i - 1, 1)

    step(i, 0)
    acc = acc_ref[...] / l_ref[...]
    lam = lam_ref[0]
    for g in range(group):
        o = acc[(2 * g) * t:(2 * g + 1) * t] - lam * acc[(2 * g + 1) * t:(2 * g + 2) * t]
        o_ref[:, g * LANES:(g + 1) * LANES] = _rms(o, gsub_ref[...]).astype(BF16)


def _diff_prompt_attn(q, k, v, bias_tiles, lam, gsub, batch, seq, t):
    nq = seq // t
    heads = q.shape[1] // LANES
    group = heads // DIFF_KV_HEADS
    gw = group * LANES
    r = group * 2 * t
    return pl.pallas_call(
        functools.partial(_diff_attn_kernel, t=t),
        grid=(batch, DIFF_KV_HEADS, nq),
        in_specs=[pl.BlockSpec(memory_space=pltpu.SMEM),
                  pl.BlockSpec((t, gw), lambda b, n, i: (b * nq + i, n)),
                  pl.BlockSpec((seq, LANES), lambda b, n, i: (b, n)),
                  pl.BlockSpec((seq, LANES), lambda b, n, i: (b, n)),
                  pl.BlockSpec((2, group, t, t), lambda b, n, i: (0, n, 0, 0)),
                  pl.BlockSpec((1, LANES), lambda b, n, i: (0, 0))],
        out_specs=pl.BlockSpec((t, gw), lambda b, n, i: (b * nq + i, n)),
        out_shape=jax.ShapeDtypeStruct((batch * seq, heads * LANES), BF16),
        scratch_shapes=[pltpu.VMEM((r, LANES), BF16), pltpu.VMEM((r, 1), F32), pltpu.VMEM((r, 1), F32),
                        pltpu.VMEM((r, LANES), F32)],
        compiler_params=_cparams(("parallel", "parallel", "arbitrary")),
        name="diff_prompt_attn",
    )(lam, q, k, v, bias_tiles, gsub)


def _diff_sample_kernel(pt_ref, lam_ref, q_ref, *rest, pages, chunk):
    k_pages = rest[:pages]
    v_pages = rest[pages:2 * pages]
    (knew_ref, vnew_ref, btail_ref, gsub_ref, o_ref,
     qs_ref, k_ref, v_ref, m_ref, l_ref, acc_ref) = rest[2 * pages:]
    j = pl.program_id(1)
    last = pl.num_programs(1) - 1
    t_new = q_ref.shape[0]
    gw = q_ref.shape[1] // DIFF_KV_HEADS
    group = gw // LANES
    rows_n = group * 2 * t_new

    @pl.when(j == 0)
    def _():
        m_ref[...] = jnp.full(m_ref.shape, -jnp.inf, F32)
        l_ref[...] = jnp.zeros(l_ref.shape, F32)
        acc_ref[...] = jnp.zeros(acc_ref.shape, F32)
        for n in range(DIFF_KV_HEADS):
            _stack_diff_queries(q_ref[:, n * gw:(n + 1) * gw], qs_ref.at[n], t_new, 0)

    def attend(n, lo, hi, bias, mask):
        s = _dot_nt(qs_ref[n].astype(BF16), k_ref[n, lo:hi, :])
        if bias is not None:
            s = s + bias
        if mask is not None:
            s = jnp.where(mask, s, -jnp.inf)
        _softmax_step(s, v_ref[n, lo:hi, :], m_ref.at[n], l_ref.at[n], acc_ref.at[n])

    for pg in range(pages):
        for n in range(DIFF_KV_HEADS):
            head_rows = pl.ds(n, PAGE_SIZE, stride=DIFF_KV_HEADS)
            k_ref[n, pg * PAGE_SIZE:(pg + 1) * PAGE_SIZE, :] = k_pages[pg][head_rows, :].astype(BF16)
            v_ref[n, pg * PAGE_SIZE:(pg + 1) * PAGE_SIZE, :] = v_pages[pg][head_rows, :].astype(BF16)
    n_chunks = pages * PAGE_SIZE // chunk
    c0 = (n_chunks - 1) * chunk
    for n in range(DIFF_KV_HEADS):
        for ci in range(n_chunks - 1):
            attend(n, ci * chunk, (ci + 1) * chunk, None, None)

    @pl.when(j != last)
    def _():
        for n in range(DIFF_KV_HEADS):
            attend(n, c0, c0 + chunk, None, None)

    @pl.when(j == last)
    def _():
        ca = c0 + chunk - PAGE_SIZE
        tq = lax.broadcasted_iota(jnp.int32, (rows_n, LANES), 0) % t_new
        tk = lax.broadcasted_iota(jnp.int32, (rows_n, LANES), 1)
        lam = lam_ref[0]
        for n in range(DIFF_KV_HEADS):
            if ca > c0:
                attend(n, c0, ca, None, None)
            attend(n, ca, ca + PAGE_SIZE, btail_ref[n, :, :PAGE_SIZE], None)
            k_ref[n, 0:LANES, :] = _pad_rows(knew_ref[:, n * LANES:(n + 1) * LANES], LANES).astype(BF16)
            v_ref[n, 0:LANES, :] = _pad_rows(vnew_ref[:, n * LANES:(n + 1) * LANES], LANES).astype(BF16)
            attend(n, 0, LANES, btail_ref[n, :, PAGE_SIZE:], tk <= tq)
            acc = acc_ref[n] / l_ref[n]
            for g in range(group):
                a = acc[2 * g * t_new:(2 * g + 2) * t_new]
                o = a[:t_new] - lam * a[t_new:]
                hd = n * group + g
                o_ref[:, hd * LANES:(hd + 1) * LANES] = _rms(o, gsub_ref[...])


def _diff_sample_attn(q, cache_k, cache_v, layer, k_new, v_new, page_table, bias_tail, lam, gsub):
    b, t_new, dq = q.shape
    n_pages = page_table.shape[1]
    pages = min(8, n_pages)
    assert n_pages % pages == 0
    steps = n_pages // pages
    chunk = min(512, pages * PAGE_SIZE)
    kvw = DIFF_KV_HEADS * LANES
    rows_n = 2 * (dq // LANES // DIFF_KV_HEADS) * t_new
    per_seq = lambda w: pl.BlockSpec((None, t_new, w), lambda i, j, pt: (i, 0, 0))

    def page_spec(pg):
        return pl.BlockSpec((None, None, PAGE_SIZE * DIFF_KV_HEADS, LANES),
                            lambda i, j, pt: (layer, pt[i * n_pages + j * pages + pg], 0, 0))

    in_specs = ([pl.BlockSpec(memory_space=pltpu.SMEM), per_seq(dq)]
                + [page_spec(pg) for pg in range(pages)] * 2
                + [per_seq(kvw), per_seq(kvw),
                   pl.BlockSpec(bias_tail.shape, lambda i, j, pt: (0, 0, 0)),
                   pl.BlockSpec((1, LANES), lambda i, j, pt: (0, 0))])
    per_head = lambda r, w, dt: pltpu.VMEM((DIFF_KV_HEADS, r, w), dt)
    grid_spec = pltpu.PrefetchScalarGridSpec(
        num_scalar_prefetch=1, grid=(b, steps), in_specs=in_specs, out_specs=per_seq(dq),
        scratch_shapes=[per_head(rows_n, LANES, F32),
                        per_head(pages * PAGE_SIZE, LANES, BF16), per_head(pages * PAGE_SIZE, LANES, BF16),
                        per_head(rows_n, 1, F32), per_head(rows_n, 1, F32), per_head(rows_n, LANES, F32)])
    return pl.pallas_call(
        functools.partial(_diff_sample_kernel, pages=pages, chunk=chunk),
        grid_spec=grid_spec,
        out_shape=jax.ShapeDtypeStruct((b, t_new, dq), F32),
        compiler_params=_cparams(("parallel", "arbitrary")),
        name="diff_sample_attn",
    )(page_table.reshape(-1), lam, q, *([cache_k] * pages), *([cache_v] * pages), k_new, v_new, bias_tail, gsub)


def _post_kernel(x_ref, o_ref, wo_ref, g_ref, win_ref, wout_ref, y_ref, *, hidden, th):
    x1 = x_ref[...] + _dot(o_ref[...].astype(BF16), wo_ref[...])
    h = _rms(x1, g_ref[...]).astype(BF16)
    acc = x1
    for c0 in range(0, hidden, th):
        gate = _dot(h, win_ref[:, c0:c0 + th])
        up = _dot(h, win_ref[:, hidden + c0:hidden + c0 + th])
        act = (gate * jax.nn.sigmoid(gate) * up).astype(BF16)
        acc = acc + _dot(act, wout_ref[c0:c0 + th, :])
    y_ref[...] = acc


def _post(x, o, w_o, g, w_in, w_out):
    n, dm = x.shape
    hidden = w_out.shape[0]
    tm = _row_tile(n, 512)
    th = 256 if hidden % 256 == 0 else LANES
    assert hidden % th == 0
    row = lambda w: pl.BlockSpec((tm, w), lambda i: (i, 0))
    resident = lambda a: pl.BlockSpec(a.shape, lambda i: (0, 0), pipeline_mode=pl.Buffered(1))
    return pl.pallas_call(
        functools.partial(_post_kernel, hidden=hidden, th=th),
        grid=(n // tm,),
        in_specs=[row(dm), row(o.shape[1]), resident(w_o), _full(g.shape), resident(w_in), resident(w_out)],
        out_specs=row(dm),
        out_shape=jax.ShapeDtypeStruct((n, dm), F32),
        compiler_params=_cparams(("parallel",)),
        name="out_proj_swiglu",
    )(x, o, w_o, g, w_in, w_out)


def _rot_half(a):
    half = a.shape[-1] // 2
    return jnp.concatenate([a[..., half:], a[..., :half]], axis=-1)


def _pad_lanes(a):
    return jnp.concatenate([a, jnp.zeros(a.shape[:-1] + (LANES - a.shape[-1],), a.dtype)], axis=-1)


def _rope_tables(pos):
    half = ROPE // 2
    inv_freq = ROPE_THETA ** (-jnp.arange(half, dtype=F32) / half)
    ang = pos.astype(F32)[:, None] * inv_freq
    cos, sin = jnp.cos(ang), jnp.sin(ang)
    return _pad_lanes(jnp.concatenate([cos, cos], -1)), _pad_lanes(jnp.concatenate([-sin, sin], -1))


def _mla_params(gmix, w_in, g_qa, g_kva, w_qb, w_kvb, g_qn, g_qr, g_kn, g_kr):
    row = lambda a: a.astype(F32).reshape(1, -1)
    w_r = w_in[:, Q_LORA + KV_LORA:]
    w_in_x = jnp.concatenate([w_in[:, :Q_LORA + KV_LORA], _pad_lanes(w_r), _pad_lanes(_rot_half(w_r))], axis=1)
    wq = w_qb.reshape(Q_LORA, MLA_HEADS, NOPE + ROPE)
    wq_r = wq[:, :, NOPE:]
    flat = lambda a: a.reshape(Q_LORA, MLA_HEADS * LANES)
    w_qb_x = jnp.concatenate([flat(wq[:, :, :NOPE]), flat(_pad_lanes(wq_r)), flat(_pad_lanes(_rot_half(wq_r)))], axis=1)
    wkv = w_kvb.reshape(KV_LORA, MLA_HEADS, NOPE + V_DIM)
    w_k = wkv[:, :, :NOPE].reshape(KV_LORA, MLA_HEADS * NOPE)
    w_v = wkv[:, :, NOPE:].reshape(KV_LORA, MLA_HEADS * V_DIM)
    return {
        "gmix": row(gmix), "w_in": w_in_x.astype(BF16), "g_qa": row(g_qa), "g_kva": row(g_kva),
        "g_kr": row(_pad_lanes(g_kr)), "g_kr_rot": row(_pad_lanes(_rot_half(g_kr))),
        "w_qb": w_qb_x.astype(BF16), "g_qn": row(g_qn) * MLA_SCALE,
        "g_qr": row(_pad_lanes(g_qr)) * MLA_SCALE, "g_qr_rot": row(_pad_lanes(_rot_half(g_qr))) * MLA_SCALE,
        "g_kn": row(g_kn), "w_k": w_k.astype(BF16), "w_k_t": w_k.T.astype(BF16), "w_v": w_v.astype(BF16),
    }


def _relative_bias(table, n_dist):
    n = jnp.arange(n_dist, dtype=jnp.int32)
    max_exact = N_BUCKETS // 2
    nf = jnp.maximum(n, 1).astype(F32)
    large = max_exact + (jnp.log(nf / max_exact) / math.log(MAX_DISTANCE / max_exact)
                         * (N_BUCKETS - max_exact)).astype(jnp.int32)
    large = jnp.minimum(large, N_BUCKETS - 1)
    bucket = jnp.where(n < max_exact, n, large)
    tab = table.astype(F32)
    return tab[bucket] - tab[N_BUCKETS - 1]


def kernel(x_prompt, x_sample, cache_mla_ckv, cache_mla_krope, cache_diff_k, cache_diff_v, page_table, rel_bias, g_mix_norm, g_ffn_norm, mla_w_in, mla_g_q_a, mla_g_kv_a, mla_w_q_b, mla_w_kv_b, mla_g_qn, mla_g_qr, mla_g_kn, mla_g_kr, mla_w_o, diff_w_qkv, diff_g_qn, diff_g_kn, diff_lq1, diff_lk1, diff_lq2, diff_lk2, diff_g_sub, diff_w_o, ffn_w_in, ffn_w_out):
    batch, seq, dm = x_prompt.shape
    dec_b, dec_t, _ = x_sample.shape
    depth = g_mix_norm.shape[0]
    n_pages = page_table.shape[1]
    past = n_pages * PAGE_SIZE
    heads = dm // (2 * DH)
    group = heads // DIFF_KV_HEADS
    kvw = DIFF_KV_HEADS * 2 * DH
    assert past >= MAX_DISTANCE

    xp = x_prompt.reshape(batch * seq, dm)
    xs = x_sample.reshape(dec_b * dec_t, dm)
    row = lambda a: a.astype(F32).reshape(1, -1)

    cos_p, sin_p = _rope_tables(jnp.arange(seq, dtype=jnp.int32))
    ts = _row_tile(dec_b * dec_t, 512)
    cos_s, sin_s = (jnp.tile(a, (ts // dec_t, 1)) for a in _rope_tables(past + jnp.arange(dec_t, dtype=jnp.int32)))

    t_diff = _row_tile(seq, 256)
    assert t_diff >= MAX_DISTANCE
    rb = _relative_bias(rel_bias, max(2 * t_diff, PAGE_SIZE + dec_t))
    a_idx = jnp.arange(t_diff)
    dist = jnp.maximum(jnp.arange(2)[:, None, None] * t_diff + a_idx[:, None] - a_idx[None, :], 0)
    bias_tiles = jnp.moveaxis(rb[dist], -1, 1)
    tq = jnp.arange(dec_t)
    d_page = PAGE_SIZE + tq[:, None] - jnp.arange(PAGE_SIZE)[None, :]
    d_new = jnp.maximum(tq[:, None] - jnp.arange(LANES)[None, :], 0)
    win = jnp.moveaxis(rb[jnp.concatenate([d_page, d_new], axis=1)], -1, 0)
    win = win.reshape(DIFF_KV_HEADS, group, 1, dec_t, 2 * LANES)
    bias_tail = jnp.broadcast_to(win, (DIFF_KV_HEADS, group, 2, dec_t, 2 * LANES))
    bias_tail = bias_tail.reshape(DIFF_KV_HEADS, -1, 2 * LANES)

    ones64 = jnp.kron(jnp.eye(LANES // DH, dtype=F32), jnp.ones((DH, DH), F32)).astype(BF16)
    ckdk = cache_diff_k.reshape(cache_diff_k.shape[:2] + (PAGE_SIZE * DIFF_KV_HEADS, 2 * DH))
    ckdv = cache_diff_v.reshape(cache_diff_v.shape[:2] + (PAGE_SIZE * DIFF_KV_HEADS, 2 * DH))

    mla_out = [[] for _ in range(4)]
    diff_out = [[] for _ in range(4)]
    for i in range(depth):
        j = i // 2
        w_ffn_in = ffn_w_in[i].astype(BF16)
        w_ffn_out = ffn_w_out[i].astype(BF16)
        g_ffn = row(g_ffn_norm[i])
        if i % 2 == 0:
            p = _mla_params(g_mix_norm[i], mla_w_in[j], mla_g_q_a[j], mla_g_kv_a[j], mla_w_q_b[j], mla_w_kv_b[j],
                            mla_g_qn[j], mla_g_qr[j], mla_g_kn[j], mla_g_kr[j])
            w_o = mla_w_o[j].astype(BF16)
            c_p, r_p, q, k, v = _mla_proj(xp, cos_p, sin_p, p, True)
            op = _mla_prompt_attn(q, k, v, batch, seq)
            c_s, r_s, qa, qr = _mla_proj(xs, cos_s, sin_s, p, False)
            shp = lambda a: a.reshape(dec_b, dec_t, a.shape[-1])
            os_ = _mla_sample_attn(shp(qa), shp(qr), cache_mla_ckv, cache_mla_krope, j, shp(c_s), shp(r_s),
                                   page_table, p)
            os_ = os_.reshape(dec_b * dec_t, -1)
            for lst, a in zip(mla_out, (c_p.reshape(batch, seq // PAGE_SIZE, PAGE_SIZE, KV_LORA),
                                        r_p.reshape(batch, seq // PAGE_SIZE, PAGE_SIZE, ROPE),
                                        shp(c_s), shp(r_s))):
                lst.append(a)
        else:
            lam_init = 0.8 - 0.6 * math.exp(-0.3 * i)
            lam = (jnp.exp(jnp.sum(diff_lq1[j].astype(F32) * diff_lk1[j].astype(F32)))
                   - jnp.exp(jnp.sum(diff_lq2[j].astype(F32) * diff_lk2[j].astype(F32))) + lam_init).reshape(1)
            gsub = row(diff_g_sub[j]) * (1.0 - lam_init)
            p = {"gmix": row(g_mix_norm[i]), "w_qkv": diff_w_qkv[j].astype(BF16),
                 "g_q": jnp.tile(row(diff_g_qn[j]), (1, LANES // DH)) * DIFF_SCALE,
                 "g_k": jnp.tile(row(diff_g_kn[j]), (1, LANES // DH)), "ones64": ones64}
            w_o = diff_w_o[j].astype(BF16)
            q, k_p, v_p, kb, vb = _diff_proj(xp, p)
            op = _diff_prompt_attn(q, kb, vb, bias_tiles, lam, gsub, batch, seq, t_diff)
            q_s, k_s, v_s, _, _ = _diff_proj(xs, p)
            shp = lambda a: a.reshape(dec_b, dec_t, a.shape[-1])
            os_ = _diff_sample_attn(shp(q_s.astype(F32)), ckdk, ckdv, j, shp(k_s), shp(v_s), page_table, bias_tail,
                                    lam, gsub)
            os_ = os_.reshape(dec_b * dec_t, -1)
            pg = lambda a: a.reshape(batch, seq // PAGE_SIZE, PAGE_SIZE, DIFF_KV_HEADS, 2 * DH)
            sm = lambda a: a.reshape(dec_b, dec_t, DIFF_KV_HEADS, 2 * DH)
            for lst, a in zip(diff_out, (pg(k_p), pg(v_p), sm(k_s), sm(v_s))):
                lst.append(a)
        xp = _post(xp, op, w_o, g_ffn, w_ffn_in, w_ffn_out)
        xs = _post(xs, os_, w_o, g_ffn, w_ffn_in, w_ffn_out)
    return (xp.reshape(batch, seq, dm), xs.reshape(dec_b, dec_t, dm),
            *(jnp.stack(l) for l in mla_out), *(jnp.stack(l) for l in diff_out))
```

```python
import functools
import math

import jax
import jax.numpy as jnp
from jax import lax
from jax.experimental import pallas as pl
from jax.experimental.pallas import tpu as pltpu

F32 = jnp.float32
BF16 = jnp.bfloat16

EPS = 1e-6
PAGE_SIZE = 128
MLA_HEADS = 8
Q_LORA = 384
KV_LORA = 256
NOPE = 128
ROPE = 64
V_DIM = 128
ROPE_THETA = 10000.0
MLA_SCALE = (NOPE + ROPE) ** -0.5
DH = 64
DIFF_KV_HEADS = 2
DIFF_SCALE = DH ** -0.5
N_BUCKETS = 32
MAX_DISTANCE = 128

LANES = 128
VMEM_LIMIT = 56 * 1024 * 1024
PROJ_ROWS = 512
DIFF_TILE = 256
PAGES_PER_STEP = 16
SAMPLE_CHUNK = 512

NT_DIMS = (((1,), (1,)), ((), ()))


def _cparams(sem):
    return pltpu.CompilerParams(dimension_semantics=sem, vmem_limit_bytes=VMEM_LIMIT)


def _rms(x, g):
    return x * lax.rsqrt(jnp.mean(x * x, axis=-1, keepdims=True) + EPS) * g


def _dot(a, b):
    return jnp.dot(a, b, preferred_element_type=F32)


def _dot_nt(a, b):
    return lax.dot_general(a, b, NT_DIMS, preferred_element_type=F32)


def _pad_rows(a, rows):
    return jnp.concatenate([a, jnp.zeros((rows - a.shape[0], a.shape[1]), a.dtype)], axis=0)


def _full(shape):
    return pl.BlockSpec(shape, lambda *_: (0,) * len(shape))


def _row_tile(n, cap):
    t = min(cap, n)
    assert n % t == 0 and t % 8 == 0
    return t


def _mla_proj_kernel(x_ref, cos_ref, sin_ref, gmix_ref, win_ref, gqa_ref, gkva_ref, gkr_ref, gkrr_ref,
                     wqb_ref, gqn_ref, gkn_ref, wk_ref, *rest, prompt):
    if prompt:
        cgq_ref, sgq_ref, wvt_ref, ckv_ref, kr_ref, qt_ref, k_ref, vt_ref = rest
    else:
        gqr_ref, gqrr_ref, ckv_ref, kr_ref, qa_ref, qr_ref = rest
    cos2, sin2 = cos_ref[...], sin_ref[...]
    h = _rms(x_ref[...], gmix_ref[...]).astype(BF16)
    d = _dot(h, win_ref[...])
    cq = _rms(d[:, :Q_LORA], gqa_ref[...]).astype(BF16)
    ckv = _rms(d[:, Q_LORA:Q_LORA + KV_LORA], gkva_ref[...])
    ckv_ref[...] = ckv
    zr = d[:, Q_LORA + KV_LORA:Q_LORA + KV_LORA + LANES]
    zrr = d[:, Q_LORA + KV_LORA + LANES:]
    inv = lax.rsqrt(jnp.sum(zr * zr, axis=-1, keepdims=True) * (1.0 / ROPE) + EPS)
    kr = (zr * gkr_ref[...] * cos2 + zrr * gkrr_ref[...] * sin2) * inv
    kr_ref[...] = kr[:, :ROPE]
    hw = MLA_HEADS * LANES
    cb = ckv.astype(BF16)

    if prompt:
        qall = _dot_nt(wqb_ref[...], cq)
        vt_ref[...] = _dot_nt(wvt_ref[...], cb).astype(BF16)
        kn_all = _dot(cb, wk_ref[...])
        krb = kr.astype(BF16)
        cgq, sgq = cgq_ref[...], sgq_ref[...]
        for hd in range(MLA_HEADS):
            c0 = hd * LANES
            zn = qall[c0:c0 + LANES]
            qn = zn * lax.rsqrt(jnp.mean(zn * zn, axis=0, keepdims=True) + EPS) * gqn_ref[...]
            zr = qall[hw + c0:hw + c0 + LANES]
            zrr = qall[2 * hw + c0:2 * hw + c0 + LANES]
            inv = lax.rsqrt(jnp.sum(zr * zr, axis=0, keepdims=True) * (1.0 / ROPE) + EPS)
            qt_ref[2 * c0:2 * c0 + LANES, :] = qn.astype(BF16)
            qt_ref[2 * c0 + LANES:2 * c0 + 2 * LANES, :] = ((zr * cgq + zrr * sgq) * inv).astype(BF16)
            kn = _rms(kn_all[:, c0:c0 + LANES], gkn_ref[...])
            k_ref[:, 2 * c0:2 * c0 + LANES] = kn.astype(BF16)
            k_ref[:, 2 * c0 + LANES:2 * c0 + 2 * LANES] = krb
    else:
        qall = _dot(cq, wqb_ref[...])
        for hd in range(MLA_HEADS):
            c0 = hd * LANES
            qn = _rms(qall[:, c0:c0 + LANES], gqn_ref[...])
            zr = qall[:, hw + c0:hw + c0 + LANES]
            zrr = qall[:, 2 * hw + c0:2 * hw + c0 + LANES]
            inv = lax.rsqrt(jnp.sum(zr * zr, axis=-1, keepdims=True) * (1.0 / ROPE) + EPS)
            qr_ref[:, c0:c0 + LANES] = (zr * gqr_ref[...] * cos2 + zrr * gqrr_ref[...] * sin2) * inv
            qg = (qn * gkn_ref[...]).astype(BF16)
            qa_ref[:, hd * KV_LORA:(hd + 1) * KV_LORA] = _dot(qg, wk_ref[c0:c0 + LANES, :])


def _mla_proj(x, cos2, sin2, p, prompt, cgq=None, sgq=None):
    n, dm = x.shape
    tm = _row_tile(n, PROJ_ROWS)
    nt = cos2.shape[0] // tm
    hw = MLA_HEADS * LANES
    row = lambda w: pl.BlockSpec((tm, w), lambda i: (i, 0))
    tab = pl.BlockSpec((tm, LANES), lambda i: (i % nt, 0))
    common = [p["gmix"], p["w_in"], p["g_qa"], p["g_kva"], p["g_kr"], p["g_kr_rot"]]
    if prompt:
        tab_t = pl.BlockSpec((LANES, tm), lambda i: (0, i % nt))
        tail = [p["w_qb_t"], p["g_qn_col"], p["g_kn"], p["w_k"]]
        ins = [x, cos2, sin2] + common + tail + [cgq, sgq, p["w_v_t"]]
        specs = ([row(dm), tab, tab] + [_full(a.shape) for a in common + tail]
                 + [tab_t, tab_t, _full(p["w_v_t"].shape)])
        feat = lambda w: pl.BlockSpec((None, w, tm), lambda i: (i, 0, 0))
        out_shape = [jax.ShapeDtypeStruct((n, KV_LORA), F32), jax.ShapeDtypeStruct((n, ROPE), F32),
                     jax.ShapeDtypeStruct((n // tm, 2 * hw, tm), BF16), jax.ShapeDtypeStruct((n, 2 * hw), BF16),
                     jax.ShapeDtypeStruct((n // tm, hw, tm), BF16)]
        out_specs = [row(KV_LORA), row(ROPE), feat(2 * hw), row(2 * hw), feat(hw)]
    else:
        tail = [p["w_qb"], p["g_qn"], p["g_kn"], p["w_k_t"], p["g_qr"], p["g_qr_rot"]]
        ins = [x, cos2, sin2] + common + tail
        specs = [row(dm), tab, tab] + [_full(a.shape) for a in common + tail]
        out_shape = [jax.ShapeDtypeStruct((n, KV_LORA), F32), jax.ShapeDtypeStruct((n, ROPE), F32),
                     jax.ShapeDtypeStruct((n, MLA_HEADS * KV_LORA), F32), jax.ShapeDtypeStruct((n, hw), F32)]
        out_specs = [row(KV_LORA), row(ROPE), row(MLA_HEADS * KV_LORA), row(hw)]
    return pl.pallas_call(
        functools.partial(_mla_proj_kernel, prompt=prompt),
        grid=(n // tm,), in_specs=specs, out_specs=out_specs, out_shape=out_shape,
        compiler_params=_cparams(("parallel",)),
        name="mla_proj_prompt" if prompt else "mla_proj_sample",
    )(*ins)


def _softmax_step_t(st, vt, m_ref, l_ref, acc_ref):
    m_old = m_ref[...]
    m_new = jnp.maximum(m_old, jnp.max(st, axis=0, keepdims=True))
    alpha = jnp.exp(m_old - m_new)
    p = jnp.exp(st - m_new)
    l_ref[...] = alpha * l_ref[...] + jnp.sum(p, axis=0, keepdims=True)
    acc_ref[...] = alpha * acc_ref[...] + _dot(vt, p.astype(BF16))
    m_ref[...] = m_new


def _init_stats(m_ref, l_ref, acc_ref):
    m_ref[...] = jnp.full(m_ref.shape, -jnp.inf, F32)
    l_ref[...] = jnp.zeros(l_ref.shape, F32)
    acc_ref[...] = jnp.zeros(acc_ref.shape, F32)


def _mla_attn_kernel(qt_ref, k_ref, vt_ref, o_ref, m_ref, l_ref, acc_ref, *, t):
    i = pl.program_id(2)
    _init_stats(m_ref, l_ref, acc_ref)
    qt = qt_ref[...]

    def scores(j):
        off = pl.multiple_of(j * t, t)
        return _dot(k_ref[pl.ds(off, t), :], qt)

    def body(j, carry):
        _softmax_step_t(scores(j), vt_ref[j], m_ref, l_ref, acc_ref)
        return carry

    lax.fori_loop(0, i, body, 0)
    keys = lax.broadcasted_iota(jnp.int32, (t, t), 0)
    queries = lax.broadcasted_iota(jnp.int32, (t, t), 1)
    _softmax_step_t(jnp.where(keys <= queries, scores(i), -jnp.inf), vt_ref[i], m_ref, l_ref, acc_ref)
    o_ref[...] = (acc_ref[...] / l_ref[...]).T.astype(BF16)


def _mla_prompt_attn(qt, k, vt, batch, seq, t):
    nq = seq // t
    kw = 2 * LANES
    return pl.pallas_call(
        functools.partial(_mla_attn_kernel, t=t),
        grid=(batch, MLA_HEADS, nq),
        in_specs=[pl.BlockSpec((None, kw, t), lambda b, h, i: (b * nq + i, h, 0)),
                  pl.BlockSpec((seq, kw), lambda b, h, i: (b, h)),
                  pl.BlockSpec((nq, V_DIM, t), lambda b, h, i: (b, h, 0))],
        out_specs=pl.BlockSpec((t, V_DIM), lambda b, h, i: (b * nq + i, h)),
        out_shape=jax.ShapeDtypeStruct((batch * seq, MLA_HEADS * V_DIM), BF16),
        scratch_shapes=[pltpu.VMEM((1, t), F32), pltpu.VMEM((1, t), F32), pltpu.VMEM((V_DIM, t), F32)],
        compiler_params=_cparams(("parallel", "parallel", "arbitrary")),
        name="mla_prompt_attn",
    )(qt, k, vt)


def _softmax_step(s, v, m_ref, l_ref, acc_ref):
    m_old = m_ref[...]
    m_new = jnp.maximum(m_old, jnp.max(s, axis=-1, keepdims=True))
    alpha = jnp.exp(m_old - m_new)
    p = jnp.exp(s - m_new)
    l_ref[...] = alpha * l_ref[...] + jnp.sum(p, axis=-1, keepdims=True)
    acc_ref[...] = alpha * acc_ref[...] + _dot(p.astype(BF16), v)
    m_ref[...] = m_new


def _mla_sample_kernel(pt_ref, qa_ref, qr_ref, *rest, pages, chunk):
    c_pages = rest[:pages]
    r_pages = rest[pages:2 * pages]
    (cnew_ref, rnew_ref, wkt_ref, wv_ref, o_ref,
     qs_ref, qrs_ref, c_ref, r_ref, m_ref, l_ref, u_ref) = rest[2 * pages:]
    j = pl.program_id(1)
    t_new = cnew_ref.shape[0]
    rows = MLA_HEADS * t_new

    @pl.when(j == 0)
    def _():
        _init_stats(m_ref, l_ref, u_ref)
        qs_ref[...] = jnp.concatenate(
            [qa_ref[:, hd * KV_LORA:(hd + 1) * KV_LORA] for hd in range(MLA_HEADS)], axis=0).astype(BF16)
        qrs_ref[...] = jnp.concatenate(
            [qr_ref[:, hd * LANES:hd * LANES + ROPE] for hd in range(MLA_HEADS)], axis=0).astype(BF16)

    def attend(c, s_rope, mask):
        n = c.shape[0]
        kt = _dot_nt(wkt_ref[...], c)
        ssq = jnp.sum((kt * kt).reshape(MLA_HEADS, NOPE, n), axis=1)
        inv = lax.rsqrt(ssq * (1.0 / NOPE) + EPS)
        s = _dot_nt(qs_ref[...], c).reshape(MLA_HEADS, t_new, n) * inv[:, None, :]
        s = s.reshape(rows, n) + s_rope
        if mask is not None:
            s = jnp.where(mask, s, -jnp.inf)
        _softmax_step(s, c, m_ref, l_ref, u_ref)

    for pg in range(pages):
        c_ref[pg * PAGE_SIZE:(pg + 1) * PAGE_SIZE, :] = c_pages[pg][...].astype(BF16)
        r_ref[:, pg * PAGE_SIZE:(pg + 1) * PAGE_SIZE] = r_pages[pg][...].astype(BF16)
    for c0 in range(0, pages * PAGE_SIZE, chunk):
        attend(c_ref[c0:c0 + chunk, :], _dot(qrs_ref[...], r_ref[:, c0:c0 + chunk]), None)

    @pl.when(j == pl.num_programs(1) - 1)
    def _():
        c_ref[0:LANES, :] = _pad_rows(cnew_ref[...], LANES).astype(BF16)
        r_new = _pad_rows(rnew_ref[...], LANES).astype(BF16)
        tq = lax.broadcasted_iota(jnp.int32, (rows, LANES), 0) % t_new
        tk = lax.broadcasted_iota(jnp.int32, (rows, LANES), 1)
        attend(c_ref[0:LANES, :], _dot_nt(qrs_ref[...], r_new), tk <= tq)
        o_all = _dot((u_ref[...] / l_ref[...]).astype(BF16), wv_ref[...])
        for hd in range(MLA_HEADS):
            o_ref[:, hd * V_DIM:(hd + 1) * V_DIM] = o_all[hd * t_new:(hd + 1) * t_new, hd * V_DIM:(hd + 1) * V_DIM]


def _sample_steps(n_pages):
    pages = min(PAGES_PER_STEP, n_pages)
    assert n_pages % pages == 0
    return pages, n_pages // pages, min(SAMPLE_CHUNK, pages * PAGE_SIZE)


def _mla_sample_attn(qa, qr, cache_c, cache_rt, layer, c_new, r_new, page_table, p):
    b, t_new, _ = qa.shape
    n_pages = page_table.shape[1]
    pages, steps, chunk = _sample_steps(n_pages)
    rows = MLA_HEADS * t_new
    per_seq = lambda w: pl.BlockSpec((None, t_new, w), lambda i, j, pt: (i, 0, 0))

    def page_spec(r, w, pg):
        return pl.BlockSpec((None, None, r, w), lambda i, j, pt: (layer, pt[i * n_pages + j * pages + pg], 0, 0))

    in_specs = ([per_seq(MLA_HEADS * KV_LORA), per_seq(MLA_HEADS * LANES)]
                + [page_spec(PAGE_SIZE, KV_LORA, pg) for pg in range(pages)]
                + [page_spec(ROPE, PAGE_SIZE, pg) for pg in range(pages)]
                + [per_seq(KV_LORA), per_seq(ROPE),
                   pl.BlockSpec(p["w_k_t"].shape, lambda i, j, pt: (0, 0)),
                   pl.BlockSpec(p["w_v"].shape, lambda i, j, pt: (0, 0))])
    grid_spec = pltpu.PrefetchScalarGridSpec(
        num_scalar_prefetch=1, grid=(b, steps), in_specs=in_specs,
        out_specs=per_seq(MLA_HEADS * V_DIM),
        scratch_shapes=[pltpu.VMEM((rows, KV_LORA), BF16), pltpu.VMEM((rows, ROPE), BF16),
                        pltpu.VMEM((pages * PAGE_SIZE, KV_LORA), BF16), pltpu.VMEM((ROPE, pages * PAGE_SIZE), BF16),
                        pltpu.VMEM((rows, 1), F32), pltpu.VMEM((rows, 1), F32), pltpu.VMEM((rows, KV_LORA), F32)])
    return pl.pallas_call(
        functools.partial(_mla_sample_kernel, pages=pages, chunk=chunk),
        grid_spec=grid_spec,
        out_shape=jax.ShapeDtypeStruct((b, t_new, MLA_HEADS * V_DIM), F32),
        compiler_params=_cparams(("parallel", "arbitrary")),
        name="mla_sample_attn",
    )(page_table.reshape(-1), qa, qr, *([cache_c] * pages), *([cache_rt] * pages), c_new, r_new,
      p["w_k_t"], p["w_v"])


def _norm64_lanes(z, g, ones):
    ms = _dot((z * z).astype(BF16), ones) * (1.0 / DH)
    return z * lax.rsqrt(ms + EPS) * g


def _diff_proj_sample_kernel(x_ref, gmix_ref, w_ref, gq_ref, gk_ref, ones_ref, q_ref, k_ref, v_ref, *, dq):
    h = _rms(x_ref[...], gmix_ref[...]).astype(BF16)
    qkv = _dot(h, w_ref[...])
    dk = k_ref.shape[1]
    for c0 in range(0, dq, LANES):
        q_ref[:, c0:c0 + LANES] = _norm64_lanes(qkv[:, c0:c0 + LANES], gq_ref[...], ones_ref[...])
    for c0 in range(0, dk, LANES):
        k_ref[:, c0:c0 + LANES] = _norm64_lanes(qkv[:, dq + c0:dq + c0 + LANES], gk_ref[...], ones_ref[...])
    v_ref[...] = qkv[:, dq + dk:]


def _diff_proj_prompt_kernel(x_ref, gmix_ref, wkv_ref, wqvt_ref, gqcol_ref, gk_ref, ones_ref,
                             k_ref, v_ref, kb_ref, qt_ref, vt_ref, *, dq, t):
    h = _rms(x_ref[...], gmix_ref[...]).astype(BF16)
    kv = _dot(h, wkv_ref[...])
    dk = k_ref.shape[1]
    for c0 in range(0, dk, LANES):
        k = _norm64_lanes(kv[:, c0:c0 + LANES], gk_ref[...], ones_ref[...])
        k_ref[:, c0:c0 + LANES] = k
        kb_ref[:, c0:c0 + LANES] = k.astype(BF16)
    v_ref[...] = kv[:, dk:]
    qv = _dot_nt(wqvt_ref[...], h)
    tm = qv.shape[1]
    zq = qv[:dq].reshape(dq // DH, DH, tm)
    ms = jnp.mean(zq * zq, axis=1, keepdims=True)
    qn = (zq * lax.rsqrt(ms + EPS) * gqcol_ref[...]).reshape(dq, tm).astype(BF16)
    vb = qv[dq:].astype(BF16)
    for c in range(tm // t):
        qt_ref[c] = qn[:, c * t:(c + 1) * t]
        vt_ref[c] = vb[:, c * t:(c + 1) * t]


def _diff_proj(x, p, prompt, t=None):
    n, dm = x.shape
    tm = _row_tile(n, PROJ_ROWS)
    dq = dm
    dk = DIFF_KV_HEADS * 2 * DH
    row = lambda w: pl.BlockSpec((tm, w), lambda i: (i, 0))
    if prompt:
        assert tm % t == 0
        ins = [x, p["gmix"], p["w_kv"], p["w_qv_t"], p["g_q_col"], p["g_k"], p["ones64"]]
        feat = lambda w: pl.BlockSpec((tm // t, w, t), lambda i: (i, 0, 0))
        return pl.pallas_call(
            functools.partial(_diff_proj_prompt_kernel, dq=dq, t=t),
            grid=(n // tm,),
            in_specs=[row(dm)] + [_full(a.shape) for a in ins[1:]],
            out_specs=[row(dk), row(dk), row(dk), feat(dq), feat(dk)],
            out_shape=[jax.ShapeDtypeStruct((n, dk), F32), jax.ShapeDtypeStruct((n, dk), F32),
                       jax.ShapeDtypeStruct((n, dk), BF16), jax.ShapeDtypeStruct((n // t, dq, t), BF16),
                       jax.ShapeDtypeStruct((n // t, dk, t), BF16)],
            compiler_params=_cparams(("parallel",)),
            name="diff_proj_prompt",
        )(*ins)
    ins = [x, p["gmix"], p["w_qkv"], p["g_q"], p["g_k"], p["ones64"]]
    return pl.pallas_call(
        functools.partial(_diff_proj_sample_kernel, dq=dq),
        grid=(n // tm,),
        in_specs=[row(dm)] + [_full(a.shape) for a in ins[1:]],
        out_specs=[row(dq), row(dk), row(dk)],
        out_shape=[jax.ShapeDtypeStruct((n, dq), F32), jax.ShapeDtypeStruct((n, dk), F32),
                   jax.ShapeDtypeStruct((n, dk), F32)],
        compiler_params=_cparams(("parallel",)),
        name="diff_proj_sample",
    )(*ins)


def _diff_attn_kernel(lam_ref, qt_ref, k_ref, vt_ref, bias_ref, gsub_ref, o_ref, qs_ref, m_ref, l_ref, acc_ref, *, t):
    i = pl.program_id(2)
    group = qt_ref.shape[0] // LANES
    _init_stats(m_ref, l_ref, acc_ref)
    feat = lax.broadcasted_iota(jnp.int32, (LANES, t), 0)
    zero = jnp.zeros((LANES, t), BF16)
    for g in range(group):
        qg = qt_ref[g * LANES:(g + 1) * LANES, :]
        qs_ref[:, (2 * g) * t:(2 * g + 1) * t] = jnp.where(feat < DH, qg, zero)
        qs_ref[:, (2 * g + 1) * t:(2 * g + 2) * t] = jnp.where(feat >= DH, qg, zero)

    def step(j, near):
        off = pl.multiple_of(j * t, t)
        st = _dot(k_ref[pl.ds(off, t), :], qs_ref[...])
        if near is not None:
            st = st + bias_ref[near]
        _softmax_step_t(st, vt_ref[j], m_ref, l_ref, acc_ref)

    def far(j, carry):
        step(j, None)
        return carry

    lax.fori_loop(0, i - 1, far, 0)

    @pl.when(i >= 1)
    def _():
        step(i - 1, 1)

    step(i, 0)
    acc = acc_ref[...] / l_ref[...]
    lam = lam_ref[0]
    for g in range(group):
        o = acc[:, (2 * g) * t:(2 * g + 1) * t] - lam * acc[:, (2 * g + 1) * t:(2 * g + 2) * t]
        o = o * lax.rsqrt(jnp.mean(o * o, axis=0, keepdims=True) + EPS) * gsub_ref[...]
        o_ref[:, g * LANES:(g + 1) * LANES] = o.T.astype(BF16)


def _diff_prompt_attn(qt, k, vt, bias_t, lam, gsub_col, batch, seq, t):
    nq = seq // t
    heads = qt.shape[1] // LANES
    group = heads // DIFF_KV_HEADS
    gw = group * LANES
    r = group * 2 * t
    return pl.pallas_call(
        functools.partial(_diff_attn_kernel, t=t),
        grid=(batch, DIFF_KV_HEADS, nq),
        in_specs=[pl.BlockSpec(memory_space=pltpu.SMEM),
                  pl.BlockSpec((None, gw, t), lambda b, n, i: (b * nq + i, n, 0)),
                  pl.BlockSpec((seq, LANES), lambda b, n, i: (b, n)),
                  pl.BlockSpec((nq, LANES, t), lambda b, n, i: (b, n, 0)),
                  pl.BlockSpec((2, None, t, r), lambda b, n, i: (0, n, 0, 0)),
                  pl.BlockSpec((LANES, 1), lambda b, n, i: (0, 0))],
        out_specs=pl.BlockSpec((t, gw), lambda b, n, i: (b * nq + i, n)),
        out_shape=jax.ShapeDtypeStruct((batch * seq, heads * LANES), BF16),
        scratch_shapes=[pltpu.VMEM((LANES, r), BF16), pltpu.VMEM((1, r), F32), pltpu.VMEM((1, r), F32),
                        pltpu.VMEM((LANES, r), F32)],
        compiler_params=_cparams(("parallel", "parallel", "arbitrary")),
        name="diff_prompt_attn",
    )(lam, qt, k, vt, bias_t, gsub_col)


def _stack_diff_queries(q, qs_ref, t):
    lane = lax.broadcasted_iota(jnp.int32, (t, LANES), 1)
    zero = jnp.zeros((t, LANES), q.dtype)
    for g in range(q.shape[1] // LANES):
        qg = q[:, g * LANES:(g + 1) * LANES]
        qs_ref[(2 * g) * t:(2 * g + 1) * t, :] = jnp.where(lane < DH, qg, zero)
        qs_ref[(2 * g + 1) * t:(2 * g + 2) * t, :] = jnp.where(lane >= DH, qg, zero)


def _diff_sample_kernel(pt_ref, lam_ref, q_ref, *rest, pages, chunk):
    k_pages = rest[:pages]
    v_pages = rest[pages:2 * pages]
    (knew_ref, vnew_ref, btail_ref, gsub_ref, o_ref,
     qs_ref, k_ref, v_ref, m_ref, l_ref, acc_ref) = rest[2 * pages:]
    j = pl.program_id(1)
    last = pl.num_programs(1) - 1
    t_new = q_ref.shape[0]
    gw = q_ref.shape[1] // DIFF_KV_HEADS
    group = gw // LANES
    rows_n = group * 2 * t_new

    @pl.when(j == 0)
    def _():
        _init_stats(m_ref, l_ref, acc_ref)
        for n in range(DIFF_KV_HEADS):
            _stack_diff_queries(q_ref[:, n * gw:(n + 1) * gw], qs_ref.at[n], t_new)

    def attend(n, lo, hi, bias, mask):
        s = _dot_nt(qs_ref[n].astype(BF16), k_ref[n, lo:hi, :])
        if bias is not None:
            s = s + bias
        if mask is not None:
            s = jnp.where(mask, s, -jnp.inf)
        _softmax_step(s, v_ref[n, lo:hi, :], m_ref.at[n], l_ref.at[n], acc_ref.at[n])

    for pg in range(pages):
        for n in range(DIFF_KV_HEADS):
            head_rows = pl.ds(n, PAGE_SIZE, stride=DIFF_KV_HEADS)
            k_ref[n, pg * PAGE_SIZE:(pg + 1) * PAGE_SIZE, :] = k_pages[pg][head_rows, :].astype(BF16)
            v_ref[n, pg * PAGE_SIZE:(pg + 1) * PAGE_SIZE, :] = v_pages[pg][head_rows, :].astype(BF16)
    n_chunks = pages * PAGE_SIZE // chunk
    c0 = (n_chunks - 1) * chunk
    for n in range(DIFF_KV_HEADS):
        for ci in range(n_chunks - 1):
            attend(n, ci * chunk, (ci + 1) * chunk, None, None)

    @pl.when(j != last)
    def _():
        for n in range(DIFF_KV_HEADS):
            attend(n, c0, c0 + chunk, None, None)

    @pl.when(j == last)
    def _():
        ca = c0 + chunk - PAGE_SIZE
        tq = lax.broadcasted_iota(jnp.int32, (rows_n, LANES), 0) % t_new
        tk = lax.broadcasted_iota(jnp.int32, (rows_n, LANES), 1)
        lam = lam_ref[0]
        for n in range(DIFF_KV_HEADS):
            if ca > c0:
                attend(n, c0, ca, None, None)
            attend(n, ca, ca + PAGE_SIZE, btail_ref[n, :, :PAGE_SIZE], None)
            k_ref[n, 0:LANES, :] = _pad_rows(knew_ref[:, n * LANES:(n + 1) * LANES], LANES).astype(BF16)
            v_ref[n, 0:LANES, :] = _pad_rows(vnew_ref[:, n * LANES:(n + 1) * LANES], LANES).astype(BF16)
            attend(n, 0, LANES, btail_ref[n, :, PAGE_SIZE:], tk <= tq)
            acc = acc_ref[n] / l_ref[n]
            for g in range(group):
                a = acc[2 * g * t_new:(2 * g + 2) * t_new]
                o = a[:t_new] - lam * a[t_new:]
                hd = n * group + g
                o_ref[:, hd * LANES:(hd + 1) * LANES] = _rms(o, gsub_ref[...])


def _diff_sample_attn(q, cache_k, cache_v, layer, k_new, v_new, page_table, bias_tail, lam, gsub):
    b, t_new, dq = q.shape
    n_pages = page_table.shape[1]
    pages, steps, chunk = _sample_steps(n_pages)
    kvw = DIFF_KV_HEADS * LANES
    rows_n = 2 * (dq // LANES // DIFF_KV_HEADS) * t_new
    per_seq = lambda w: pl.BlockSpec((None, t_new, w), lambda i, j, pt: (i, 0, 0))

    def page_spec(pg):
        return pl.BlockSpec((None, None, PAGE_SIZE * DIFF_KV_HEADS, LANES),
                            lambda i, j, pt: (layer, pt[i * n_pages + j * pages + pg], 0, 0))

    in_specs = ([pl.BlockSpec(memory_space=pltpu.SMEM), per_seq(dq)]
                + [page_spec(pg) for pg in range(pages)] * 2
                + [per_seq(kvw), per_seq(kvw),
                   pl.BlockSpec(bias_tail.shape, lambda i, j, pt: (0, 0, 0)),
                   pl.BlockSpec((1, LANES), lambda i, j, pt: (0, 0))])
    per_head = lambda r, w, dt: pltpu.VMEM((DIFF_KV_HEADS, r, w), dt)
    grid_spec = pltpu.PrefetchScalarGridSpec(
        num_scalar_prefetch=1, grid=(b, steps), in_specs=in_specs, out_specs=per_seq(dq),
        scratch_shapes=[per_head(rows_n, LANES, F32),
                        per_head(pages * PAGE_SIZE, LANES, BF16), per_head(pages * PAGE_SIZE, LANES, BF16),
                        per_head(rows_n, 1, F32), per_head(rows_n, 1, F32), per_head(rows_n, LANES, F32)])
    return pl.pallas_call(
        functools.partial(_diff_sample_kernel, pages=pages, chunk=chunk),
        grid_spec=grid_spec,
        out_shape=jax.ShapeDtypeStruct((b, t_new, dq), F32),
        compiler_params=_cparams(("parallel", "arbitrary")),
        name="diff_sample_attn",
    )(page_table.reshape(-1), lam, q, *([cache_k] * pages), *([cache_v] * pages), k_new, v_new, bias_tail, gsub)


def _post_kernel(x_ref, o_ref, wo_ref, g_ref, win_ref, wout_ref, y_ref, *, hidden, th):
    x1 = x_ref[...] + _dot(o_ref[...].astype(BF16), wo_ref[...])
    h = _rms(x1, g_ref[...]).astype(BF16)
    acc = x1
    for c0 in range(0, hidden, th):
        gate = _dot(h, win_ref[:, c0:c0 + th])
        up = _dot(h, win_ref[:, hidden + c0:hidden + c0 + th])
        act = (gate * jax.nn.sigmoid(gate) * up).astype(BF16)
        acc = acc + _dot(act, wout_ref[c0:c0 + th, :])
    y_ref[...] = acc


def _post(x, o, w_o, g, w_in, w_out):
    n, dm = x.shape
    hidden = w_out.shape[0]
    tm = _row_tile(n, PROJ_ROWS)
    th = 256 if hidden % 256 == 0 else LANES
    assert hidden % th == 0
    row = lambda w: pl.BlockSpec((tm, w), lambda i: (i, 0))
    resident = lambda a: pl.BlockSpec(a.shape, lambda i: (0, 0), pipeline_mode=pl.Buffered(1))
    return pl.pallas_call(
        functools.partial(_post_kernel, hidden=hidden, th=th),
        grid=(n // tm,),
        in_specs=[row(dm), row(o.shape[1]), resident(w_o), _full(g.shape), resident(w_in), resident(w_out)],
        out_specs=row(dm),
        out_shape=jax.ShapeDtypeStruct((n, dm), F32),
        compiler_params=_cparams(("parallel",)),
        name="out_proj_swiglu",
    )(x, o, w_o, g, w_in, w_out)


def _rot_half(a):
    half = a.shape[-1] // 2
    return jnp.concatenate([a[..., half:], a[..., :half]], axis=-1)


def _pad_lanes(a):
    return jnp.concatenate([a, jnp.zeros(a.shape[:-1] + (LANES - a.shape[-1],), a.dtype)], axis=-1)


def _rope_tables(pos):
    half = ROPE // 2
    inv_freq = ROPE_THETA ** (-jnp.arange(half, dtype=F32) / half)
    ang = pos.astype(F32)[:, None] * inv_freq
    cos, sin = jnp.cos(ang), jnp.sin(ang)
    return _pad_lanes(jnp.concatenate([cos, cos], -1)), _pad_lanes(jnp.concatenate([-sin, sin], -1))


def _mla_params(gmix, w_in, g_qa, g_kva, w_qb, w_kvb, g_qn, g_qr, g_kn, g_kr):
    row = lambda a: a.astype(F32).reshape(1, -1)
    w_r = w_in[:, Q_LORA + KV_LORA:]
    w_in_x = jnp.concatenate([w_in[:, :Q_LORA + KV_LORA], _pad_lanes(w_r), _pad_lanes(_rot_half(w_r))], axis=1)
    wq = w_qb.reshape(Q_LORA, MLA_HEADS, NOPE + ROPE)
    wq_r = wq[:, :, NOPE:]
    flat = lambda a: a.reshape(Q_LORA, MLA_HEADS * LANES)
    w_qb_x = jnp.concatenate([flat(wq[:, :, :NOPE]), flat(_pad_lanes(wq_r)), flat(_pad_lanes(_rot_half(wq_r)))], axis=1)
    wkv = w_kvb.reshape(KV_LORA, MLA_HEADS, NOPE + V_DIM)
    w_k = wkv[:, :, :NOPE].reshape(KV_LORA, MLA_HEADS * NOPE)
    w_v = wkv[:, :, NOPE:].reshape(KV_LORA, MLA_HEADS * V_DIM)
    return {
        "gmix": row(gmix), "w_in": w_in_x.astype(BF16), "g_qa": row(g_qa), "g_kva": row(g_kva),
        "g_kr": row(_pad_lanes(g_kr)), "g_kr_rot": row(_pad_lanes(_rot_half(g_kr))),
        "w_qb": w_qb_x.astype(BF16), "w_qb_t": w_qb_x.T.astype(BF16),
        "g_qn": row(g_qn) * MLA_SCALE, "g_qn_col": (g_qn.astype(F32) * MLA_SCALE).reshape(-1, 1),
        "g_qr": row(_pad_lanes(g_qr)) * MLA_SCALE, "g_qr_rot": row(_pad_lanes(_rot_half(g_qr))) * MLA_SCALE,
        "g_kn": row(g_kn), "w_k": w_k.astype(BF16), "w_k_t": w_k.T.astype(BF16),
        "w_v": w_v.astype(BF16), "w_v_t": w_v.T.astype(BF16),
    }


def _relative_bias(table, n_dist):
    n = jnp.arange(n_dist, dtype=jnp.int32)
    max_exact = N_BUCKETS // 2
    nf = jnp.maximum(n, 1).astype(F32)
    large = max_exact + (jnp.log(nf / max_exact) / math.log(MAX_DISTANCE / max_exact)
                         * (N_BUCKETS - max_exact)).astype(jnp.int32)
    large = jnp.minimum(large, N_BUCKETS - 1)
    bucket = jnp.where(n < max_exact, n, large)
    tab = table.astype(F32)
    return tab[bucket] - tab[N_BUCKETS - 1]


def _bias_tiles_t(rb, t, group):
    heads = rb.shape[1]
    length = 3 * t - 1
    f = jnp.concatenate([jnp.full((t - 1, heads), -jnp.inf, F32), rb[:2 * t]], axis=0).T
    tiles = []
    for d in range(2):
        w = jnp.roll(f, -(d * t + t - 1), axis=1)
        z = jnp.tile(w, (1, t))[:, :t * (length - 1)].reshape(heads, t, length - 1)[:, :, :t]
        tiles.append(z)
    z = jnp.stack(tiles).reshape(2, DIFF_KV_HEADS, group, 1, t, t)
    z = jnp.broadcast_to(z, (2, DIFF_KV_HEADS, group, 2, t, t))
    return jnp.moveaxis(z, 4, 2).reshape(2, DIFF_KV_HEADS, t, group * 2 * t)


def kernel(x_prompt, x_sample, cache_mla_ckv, cache_mla_krope, cache_diff_k, cache_diff_v, page_table, rel_bias, g_mix_norm, g_ffn_norm, mla_w_in, mla_g_q_a, mla_g_kv_a, mla_w_q_b, mla_w_kv_b, mla_g_qn, mla_g_qr, mla_g_kn, mla_g_kr, mla_w_o, diff_w_qkv, diff_g_qn, diff_g_kn, diff_lq1, diff_lk1, diff_lq2, diff_lk2, diff_g_sub, diff_w_o, ffn_w_in, ffn_w_out):
    batch, seq, dm = x_prompt.shape
    dec_b, dec_t, _ = x_sample.shape
    depth = g_mix_norm.shape[0]
    n_pages = page_table.shape[1]
    past = n_pages * PAGE_SIZE
    heads = dm // (2 * DH)
    group = heads // DIFF_KV_HEADS
    dk = DIFF_KV_HEADS * 2 * DH
    assert past >= MAX_DISTANCE

    xp = x_prompt.reshape(batch * seq, dm)
    xs = x_sample.reshape(dec_b * dec_t, dm)
    row = lambda a: a.astype(F32).reshape(1, -1)

    t_mla = _row_tile(seq, PROJ_ROWS)
    cos_p, sin_p = _rope_tables(jnp.arange(seq, dtype=jnp.int32))
    ts = _row_tile(dec_b * dec_t, PROJ_ROWS)
    cos_s, sin_s = (jnp.tile(a, (ts // dec_t, 1)) for a in _rope_tables(past + jnp.arange(dec_t, dtype=jnp.int32)))

    t_diff = _row_tile(seq, DIFF_TILE)
    assert t_diff >= MAX_DISTANCE
    rb = _relative_bias(rel_bias, max(2 * t_diff, PAGE_SIZE + dec_t))
    bias_t = _bias_tiles_t(rb, t_diff, group)
    tq = jnp.arange(dec_t)
    d_page = PAGE_SIZE + tq[:, None] - jnp.arange(PAGE_SIZE)[None, :]
    d_new = jnp.maximum(tq[:, None] - jnp.arange(LANES)[None, :], 0)
    win = jnp.moveaxis(rb[jnp.concatenate([d_page, d_new], axis=1)], -1, 0)
    win = win.reshape(DIFF_KV_HEADS, group, 1, dec_t, 2 * LANES)
    bias_tail = jnp.broadcast_to(win, (DIFF_KV_HEADS, group, 2, dec_t, 2 * LANES))
    bias_tail = bias_tail.reshape(DIFF_KV_HEADS, -1, 2 * LANES)

    ones64 = jnp.kron(jnp.eye(LANES // DH, dtype=F32), jnp.ones((DH, DH), F32)).astype(BF16)
    ckdk = cache_diff_k.reshape(cache_diff_k.shape[:2] + (PAGE_SIZE * DIFF_KV_HEADS, 2 * DH))
    ckdv = cache_diff_v.reshape(cache_diff_v.shape[:2] + (PAGE_SIZE * DIFF_KV_HEADS, 2 * DH))
    ckrt = jnp.swapaxes(cache_mla_krope, 2, 3)

    mla_out = [[] for _ in range(4)]
    diff_out = [[] for _ in range(4)]
    shp = lambda a: a.reshape(dec_b, dec_t, a.shape[-1])
    for i in range(depth):
        j = i // 2
        w_ffn_in = ffn_w_in[i].astype(BF16)
        w_ffn_out = ffn_w_out[i].astype(BF16)
        g_ffn = row(g_ffn_norm[i])
        if i % 2 == 0:
            p = _mla_params(g_mix_norm[i], mla_w_in[j], mla_g_q_a[j], mla_g_kv_a[j], mla_w_q_b[j], mla_w_kv_b[j],
                            mla_g_qn[j], mla_g_qr[j], mla_g_kn[j], mla_g_kr[j])
            w_o = mla_w_o[j].astype(BF16)
            cgq = (cos_p * p["g_qr"]).T
            sgq = (sin_p * p["g_qr_rot"]).T
            c_p, r_p, qt, k, vt = _mla_proj(xp, cos_p, sin_p, p, True, cgq, sgq)
            op = _mla_prompt_attn(qt, k, vt, batch, seq, t_mla)
            c_s, r_s, qa, qr = _mla_proj(xs, cos_s, sin_s, p, False)
            os_ = _mla_sample_attn(shp(qa), shp(qr), cache_mla_ckv, ckrt, j, shp(c_s), shp(r_s), page_table, p)
            os_ = os_.reshape(dec_b * dec_t, -1)
            for lst, a in zip(mla_out, (c_p.reshape(batch, seq // PAGE_SIZE, PAGE_SIZE, KV_LORA),
                                        r_p.reshape(batch, seq // PAGE_SIZE, PAGE_SIZE, ROPE),
                                        shp(c_s), shp(r_s))):
                lst.append(a)
        else:
            lam_init = 0.8 - 0.6 * math.exp(-0.3 * i)
            lam = (jnp.exp(jnp.sum(diff_lq1[j].astype(F32) * diff_lk1[j].astype(F32)))
                   - jnp.exp(jnp.sum(diff_lq2[j].astype(F32) * diff_lk2[j].astype(F32))) + lam_init).reshape(1)
            gsub = row(diff_g_sub[j]) * (1.0 - lam_init)
            w = diff_w_qkv[j]
            p = {"gmix": row(g_mix_norm[i]), "w_qkv": w.astype(BF16), "w_kv": w[:, dm:].astype(BF16),
                 "w_qv_t": jnp.concatenate([w[:, :dm], w[:, dm + dk:]], axis=1).T.astype(BF16),
                 "g_q": jnp.tile(row(diff_g_qn[j]), (1, LANES // DH)) * DIFF_SCALE,
                 "g_q_col": (diff_g_qn[j].astype(F32) * DIFF_SCALE).reshape(-1, 1),
                 "g_k": jnp.tile(row(diff_g_kn[j]), (1, LANES // DH)), "ones64": ones64}
            w_o = diff_w_o[j].astype(BF16)
            k_p, v_p, kb, qt, vt = _diff_proj(xp, p, True, t_diff)
            op = _diff_prompt_attn(qt, kb, vt, bias_t, lam, gsub.reshape(-1, 1), batch, seq, t_diff)
            q_s, k_s, v_s = _diff_proj(xs, p, False)
            os_ = _diff_sample_attn(shp(q_s), ckdk, ckdv, j, shp(k_s), shp(v_s), page_table, bias_tail, lam, gsub)
            os_ = os_.reshape(dec_b * dec_t, -1)
            pg = lambda a: a.reshape(batch, seq // PAGE_SIZE, PAGE_SIZE, DIFF_KV_HEADS, 2 * DH)
            sm = lambda a: a.reshape(dec_b, dec_t, DIFF_KV_HEADS, 2 * DH)
            for lst, a in zip(diff_out, (pg(k_p), pg(v_p), sm(k_s), sm(v_s))):
                lst.append(a)
        xp = _post(xp, op, w_o, g_ffn, w_ffn_in, w_ffn_out)
        xs = _post(xs, os_, w_o, g_ffn, w_ffn_in, w_ffn_out)
    return (xp.reshape(batch, seq, dm), xs.reshape(dec_b, dec_t, dm),
            *(jnp.stack(l) for l in mla_out), *(jnp.stack(l) for l in diff_out))
```

```python
import functools
import math

import jax
import jax.numpy as jnp
from jax import lax
from jax.experimental import pallas as pl
from jax.experimental.pallas import tpu as pltpu

F32 = jnp.float32
BF16 = jnp.bfloat16

EPS = 1e-6
PAGE_SIZE = 128
MLA_HEADS = 8
Q_LORA = 384
KV_LORA = 256
NOPE = 128
ROPE = 64
V_DIM = 128
ROPE_THETA = 10000.0
MLA_SCALE = (NOPE + ROPE) ** -0.5
DH = 64
DIFF_KV_HEADS = 2
DIFF_SCALE = DH ** -0.5
N_BUCKETS = 32
MAX_DISTANCE = 128

LANES = 128
VMEM_LIMIT = 56 * 1024 * 1024
PROJ_ROWS = 512
DIFF_TILE = 256
V_ROWS = V_DIM + 16
MLA_PAGES_PER_STEP = 32
DIFF_PAGES_PER_STEP = 32
NORM_CHUNK = 512
LOG2E = math.log2(math.e)

NT_DIMS = (((1,), (1,)), ((), ()))


def _cparams(sem):
    return pltpu.CompilerParams(dimension_semantics=sem, vmem_limit_bytes=VMEM_LIMIT)


def _rms(x, g):
    return x * lax.rsqrt(jnp.mean(x * x, axis=-1, keepdims=True) + EPS) * g


def _dot(a, b):
    return jnp.dot(a, b, preferred_element_type=F32)


def _dot_nt(a, b):
    return lax.dot_general(a, b, NT_DIMS, preferred_element_type=F32)


def _pad_rows(a, rows):
    return jnp.concatenate([a, jnp.zeros((rows - a.shape[0], a.shape[1]), a.dtype)], axis=0)


def _full(shape):
    return pl.BlockSpec(shape, lambda *_: (0,) * len(shape))


def _row_tile(n, cap):
    t = min(cap, n)
    assert n % t == 0 and t % 8 == 0
    return t


def _mla_proj_kernel(x_ref, cos_ref, sin_ref, gmix_ref, win_ref, gqa_ref, gkva_ref, gkr_ref, gkrr_ref,
                     wqb_ref, gqn_ref, gkn_ref, wk_ref, *rest, prompt):
    if prompt:
        cgq_ref, sgq_ref, wvt_ref, ckv_ref, kr_ref, qt_ref, k_ref, vt_ref = rest
    else:
        gqr_ref, gqrr_ref, ckv_ref, kr_ref, qa_ref, qr_ref = rest
    cos2, sin2 = cos_ref[...], sin_ref[...]
    h = _rms(x_ref[...], gmix_ref[...]).astype(BF16)
    d = _dot(h, win_ref[...])
    cq = _rms(d[:, :Q_LORA], gqa_ref[...]).astype(BF16)
    ckv = _rms(d[:, Q_LORA:Q_LORA + KV_LORA], gkva_ref[...])
    ckv_ref[...] = ckv
    zr = d[:, Q_LORA + KV_LORA:Q_LORA + KV_LORA + LANES]
    zrr = d[:, Q_LORA + KV_LORA + LANES:]
    inv = lax.rsqrt(jnp.sum(zr * zr, axis=-1, keepdims=True) * (1.0 / ROPE) + EPS)
    kr = (zr * gkr_ref[...] * cos2 + zrr * gkrr_ref[...] * sin2) * inv
    kr_ref[...] = kr[:, :ROPE]
    hw = MLA_HEADS * LANES
    cb = ckv.astype(BF16)

    if prompt:
        qall = _dot_nt(wqb_ref[...], cq)
        vt_ref[...] = _with_ones_rows(_dot_nt(wvt_ref[...], cb).astype(BF16))
        kn_all = _dot(cb, wk_ref[...])
        krb = kr.astype(BF16)
        cgq, sgq = cgq_ref[...], sgq_ref[...]
        for hd in range(MLA_HEADS):
            c0 = hd * LANES
            zn = qall[c0:c0 + LANES]
            qn = zn * lax.rsqrt(jnp.mean(zn * zn, axis=0, keepdims=True) + EPS) * gqn_ref[...]
            zr = qall[hw + c0:hw + c0 + LANES]
            zrr = qall[2 * hw + c0:2 * hw + c0 + LANES]
            inv = lax.rsqrt(jnp.sum(zr * zr, axis=0, keepdims=True) * (1.0 / ROPE) + EPS)
            qt_ref[2 * c0:2 * c0 + LANES, :] = qn.astype(BF16)
            qt_ref[2 * c0 + LANES:2 * c0 + 2 * LANES, :] = ((zr * cgq + zrr * sgq) * inv).astype(BF16)
            kn = _rms(kn_all[:, c0:c0 + LANES], gkn_ref[...])
            k_ref[:, 2 * c0:2 * c0 + LANES] = kn.astype(BF16)
            k_ref[:, 2 * c0 + LANES:2 * c0 + 2 * LANES] = krb
    else:
        qall = _dot(cq, wqb_ref[...])
        for hd in range(MLA_HEADS):
            c0 = hd * LANES
            qn = _rms(qall[:, c0:c0 + LANES], gqn_ref[...])
            zr = qall[:, hw + c0:hw + c0 + LANES]
            zrr = qall[:, 2 * hw + c0:2 * hw + c0 + LANES]
            inv = lax.rsqrt(jnp.sum(zr * zr, axis=-1, keepdims=True) * (1.0 / ROPE) + EPS)
            qr_ref[:, c0:c0 + LANES] = (zr * gqr_ref[...] * cos2 + zrr * gqrr_ref[...] * sin2) * inv
            qg = (qn * gkn_ref[...]).astype(BF16)
            qa_ref[:, hd * KV_LORA:(hd + 1) * KV_LORA] = _dot(qg, wk_ref[c0:c0 + LANES, :])


def _mla_proj(x, cos2, sin2, p, prompt, cgq=None, sgq=None):
    n, dm = x.shape
    tm = _row_tile(n, PROJ_ROWS)
    nt = cos2.shape[0] // tm
    hw = MLA_HEADS * LANES
    row = lambda w: pl.BlockSpec((tm, w), lambda i: (i, 0))
    tab = pl.BlockSpec((tm, LANES), lambda i: (i % nt, 0))
    common = [p["gmix"], p["w_in"], p["g_qa"], p["g_kva"], p["g_kr"], p["g_kr_rot"]]
    if prompt:
        tab_t = pl.BlockSpec((LANES, tm), lambda i: (0, i % nt))
        tail = [p["w_qb_t"], p["g_qn_col"], p["g_kn"], p["w_k"]]
        ins = [x, cos2, sin2] + common + tail + [cgq, sgq, p["w_v_t"]]
        specs = ([row(dm), tab, tab] + [_full(a.shape) for a in common + tail]
                 + [tab_t, tab_t, _full(p["w_v_t"].shape)])
        feat = lambda w: pl.BlockSpec((None, w, tm), lambda i: (i, 0, 0))
        out_shape = [jax.ShapeDtypeStruct((n, KV_LORA), F32), jax.ShapeDtypeStruct((n, ROPE), F32),
                     jax.ShapeDtypeStruct((n // tm, 2 * hw, tm), BF16), jax.ShapeDtypeStruct((n, 2 * hw), BF16),
                     jax.ShapeDtypeStruct((n // tm, MLA_HEADS * V_ROWS, tm), BF16)]
        out_specs = [row(KV_LORA), row(ROPE), feat(2 * hw), row(2 * hw), feat(MLA_HEADS * V_ROWS)]
    else:
        tail = [p["w_qb"], p["g_qn"], p["g_kn"], p["w_k_t"], p["g_qr"], p["g_qr_rot"]]
        ins = [x, cos2, sin2] + common + tail
        specs = [row(dm), tab, tab] + [_full(a.shape) for a in common + tail]
        out_shape = [jax.ShapeDtypeStruct((n, KV_LORA), F32), jax.ShapeDtypeStruct((n, ROPE), F32),
                     jax.ShapeDtypeStruct((n, MLA_HEADS * KV_LORA), F32), jax.ShapeDtypeStruct((n, hw), F32)]
        out_specs = [row(KV_LORA), row(ROPE), row(MLA_HEADS * KV_LORA), row(hw)]
    return pl.pallas_call(
        functools.partial(_mla_proj_kernel, prompt=prompt),
        grid=(n // tm,), in_specs=specs, out_specs=out_specs, out_shape=out_shape,
        compiler_params=_cparams(("parallel",)),
        name="mla_proj_prompt" if prompt else "mla_proj_sample",
    )(*ins)


def _softmax_step_t(st, vt, m_ref, acc_ref):
    m_old = m_ref[...]
    m_new = jnp.maximum(m_old, jnp.max(st, axis=0, keepdims=True))
    p = jnp.exp2(st - m_new).astype(BF16)
    acc_ref[...] = jnp.exp2(m_old - m_new) * acc_ref[...] + _dot(vt, p)
    m_ref[...] = m_new


def _with_ones_rows(vt_heads):
    n = vt_heads.shape[1]
    ones = jnp.ones((V_ROWS - V_DIM, n), BF16)
    parts = []
    for h0 in range(0, vt_heads.shape[0], V_DIM):
        parts += [vt_heads[h0:h0 + V_DIM], ones]
    return jnp.concatenate(parts, axis=0)


def _init_stats(m_ref, l_ref, acc_ref):
    m_ref[...] = jnp.full(m_ref.shape, -jnp.inf, F32)
    if l_ref is not None:
        l_ref[...] = jnp.zeros(l_ref.shape, F32)
    acc_ref[...] = jnp.zeros(acc_ref.shape, F32)


def _mla_attn_kernel(qt_ref, k_ref, vt_ref, o_ref, m_ref, acc_ref, *, t):
    i = pl.program_id(2)
    _init_stats(m_ref, None, acc_ref)
    qt = qt_ref[...]

    def scores(j):
        off = pl.multiple_of(j * t, t)
        return _dot(k_ref[pl.ds(off, t), :], qt)

    def body(j, carry):
        _softmax_step_t(scores(j), vt_ref[j], m_ref, acc_ref)
        return carry

    lax.fori_loop(0, i, body, 0)
    keys = lax.broadcasted_iota(jnp.int32, (t, t), 0)
    queries = lax.broadcasted_iota(jnp.int32, (t, t), 1)
    _softmax_step_t(jnp.where(keys <= queries, scores(i), -jnp.inf), vt_ref[i], m_ref, acc_ref)
    o_ref[...] = (acc_ref[0:V_DIM, :] / acc_ref[V_DIM:V_DIM + 1, :]).T.astype(BF16)


def _mla_prompt_attn(qt, k, vt, batch, seq, t):
    nq = seq // t
    kw = 2 * LANES
    return pl.pallas_call(
        functools.partial(_mla_attn_kernel, t=t),
        grid=(batch, MLA_HEADS, nq),
        in_specs=[pl.BlockSpec((None, kw, t), lambda b, h, i: (b * nq + i, h, 0)),
                  pl.BlockSpec((seq, kw), lambda b, h, i: (b, h)),
                  pl.BlockSpec((nq, V_ROWS, t), lambda b, h, i: (b, h, 0))],
        out_specs=pl.BlockSpec((t, V_DIM), lambda b, h, i: (b * nq + i, h)),
        out_shape=jax.ShapeDtypeStruct((batch * seq, MLA_HEADS * V_DIM), BF16),
        scratch_shapes=[pltpu.VMEM((1, t), F32), pltpu.VMEM((V_ROWS, t), F32)],
        compiler_params=_cparams(("parallel", "parallel", "arbitrary")),
        name="mla_prompt_attn",
    )(qt, k, vt)


def _softmax_step(s, v, m_ref, l_ref, acc_ref):
    m_old = m_ref[...]
    m_new = jnp.maximum(m_old, jnp.max(s, axis=-1, keepdims=True))
    alpha = jnp.exp(m_old - m_new)
    p = jnp.exp(s - m_new)
    l_ref[...] = alpha * l_ref[...] + jnp.sum(p, axis=-1, keepdims=True)
    acc_ref[...] = alpha * acc_ref[...] + _dot(p.astype(BF16), v)
    m_ref[...] = m_new


def _mla_sample_kernel(pt_ref, qa_ref, qr_ref, *rest, pages, chunk):
    c_pages = rest[:pages]
    r_pages = rest[pages:2 * pages]
    (cnew_ref, rnew_ref, wkt_ref, wv_ref, o_ref,
     qs_ref, qrs_ref, c_ref, r_ref, inv_ref, m_ref, l_ref, u_ref) = rest[2 * pages:]
    j = pl.program_id(1)
    t_new = cnew_ref.shape[0]
    rows = MLA_HEADS * t_new
    n_tok = pages * PAGE_SIZE

    @pl.when(j == 0)
    def _():
        _init_stats(m_ref, l_ref, u_ref)
        qs_ref[...] = jnp.concatenate(
            [qa_ref[:, hd * KV_LORA:(hd + 1) * KV_LORA] for hd in range(MLA_HEADS)], axis=0).astype(BF16)
        qrs_ref[...] = jnp.concatenate(
            [qr_ref[:, hd * LANES:hd * LANES + ROPE] for hd in range(MLA_HEADS)], axis=0).astype(BF16)

    def key_norms(lo, hi):
        kt = _dot_nt(wkt_ref[...], c_ref[lo:hi, :])
        ssq = jnp.sum((kt * kt).reshape(MLA_HEADS, NOPE, hi - lo), axis=1)
        inv_ref[:, lo:hi] = lax.rsqrt(ssq * (1.0 / NOPE) + EPS)

    def attend(n, s_rope, mask):
        c = c_ref[0:n, :]
        s = _dot_nt(qs_ref[...], c).reshape(MLA_HEADS, t_new, n) * inv_ref[:, 0:n][:, None, :]
        s = s.reshape(rows, n) + s_rope
        if mask is not None:
            s = jnp.where(mask, s, -jnp.inf)
        _softmax_step(s, c, m_ref, l_ref, u_ref)

    for pg in range(pages):
        c_ref[pg * PAGE_SIZE:(pg + 1) * PAGE_SIZE, :] = c_pages[pg][...].astype(BF16)
        r_ref[:, pg * PAGE_SIZE:(pg + 1) * PAGE_SIZE] = r_pages[pg][...].astype(BF16)
    for c0 in range(0, n_tok, chunk):
        key_norms(c0, c0 + chunk)
    attend(n_tok, _dot(qrs_ref[...], r_ref[...]), None)

    @pl.when(j == pl.num_programs(1) - 1)
    def _():
        c_ref[0:LANES, :] = _pad_rows(cnew_ref[...], LANES).astype(BF16)
        r_new = _pad_rows(rnew_ref[...], LANES).astype(BF16)
        tq = lax.broadcasted_iota(jnp.int32, (rows, LANES), 0) % t_new
        tk = lax.broadcasted_iota(jnp.int32, (rows, LANES), 1)
        key_norms(0, LANES)
        attend(LANES, _dot_nt(qrs_ref[...], r_new), tk <= tq)
        o_all = _dot((u_ref[...] / l_ref[...]).astype(BF16), wv_ref[...])
        for hd in range(MLA_HEADS):
            o_ref[:, hd * V_DIM:(hd + 1) * V_DIM] = o_all[hd * t_new:(hd + 1) * t_new, hd * V_DIM:(hd + 1) * V_DIM]


def _sample_steps(n_pages, cap):
    pages = min(cap, n_pages)
    assert n_pages % pages == 0
    return pages, n_pages // pages


def _mla_sample_attn(qa, qr, cache_c, cache_rt, layer, c_new, r_new, page_table, p):
    b, t_new, _ = qa.shape
    n_pages = page_table.shape[1]
    pages, steps = _sample_steps(n_pages, MLA_PAGES_PER_STEP)
    chunk = min(NORM_CHUNK, pages * PAGE_SIZE)
    rows = MLA_HEADS * t_new
    per_seq = lambda w: pl.BlockSpec((None, t_new, w), lambda i, j, pt: (i, 0, 0))

    def page_spec(r, w, pg):
        return pl.BlockSpec((None, None, r, w), lambda i, j, pt: (layer, pt[i * n_pages + j * pages + pg], 0, 0))

    in_specs = ([per_seq(MLA_HEADS * KV_LORA), per_seq(MLA_HEADS * LANES)]
                + [page_spec(PAGE_SIZE, KV_LORA, pg) for pg in range(pages)]
                + [page_spec(ROPE, PAGE_SIZE, pg) for pg in range(pages)]
                + [per_seq(KV_LORA), per_seq(ROPE),
                   pl.BlockSpec(p["w_k_t"].shape, lambda i, j, pt: (0, 0)),
                   pl.BlockSpec(p["w_v"].shape, lambda i, j, pt: (0, 0))])
    grid_spec = pltpu.PrefetchScalarGridSpec(
        num_scalar_prefetch=1, grid=(b, steps), in_specs=in_specs,
        out_specs=per_seq(MLA_HEADS * V_DIM),
        scratch_shapes=[pltpu.VMEM((rows, KV_LORA), BF16), pltpu.VMEM((rows, ROPE), BF16),
                        pltpu.VMEM((pages * PAGE_SIZE, KV_LORA), BF16), pltpu.VMEM((ROPE, pages * PAGE_SIZE), BF16),
                        pltpu.VMEM((MLA_HEADS, pages * PAGE_SIZE), F32),
                        pltpu.VMEM((rows, 1), F32), pltpu.VMEM((rows, 1), F32), pltpu.VMEM((rows, KV_LORA), F32)])
    return pl.pallas_call(
        functools.partial(_mla_sample_kernel, pages=pages, chunk=chunk),
        grid_spec=grid_spec,
        out_shape=jax.ShapeDtypeStruct((b, t_new, MLA_HEADS * V_DIM), F32),
        compiler_params=_cparams(("parallel", "arbitrary")),
        name="mla_sample_attn",
    )(page_table.reshape(-1), qa, qr, *([cache_c] * pages), *([cache_rt] * pages), c_new, r_new,
      p["w_k_t"], p["w_v"])


def _norm64_lanes(z, g, ones):
    ms = _dot((z * z).astype(BF16), ones) * (1.0 / DH)
    return z * lax.rsqrt(ms + EPS) * g


def _diff_proj_sample_kernel(x_ref, gmix_ref, w_ref, gq_ref, gk_ref, ones_ref, q_ref, k_ref, v_ref, *, dq):
    h = _rms(x_ref[...], gmix_ref[...]).astype(BF16)
    qkv = _dot(h, w_ref[...])
    dk = k_ref.shape[1]
    for c0 in range(0, dq, LANES):
        q_ref[:, c0:c0 + LANES] = _norm64_lanes(qkv[:, c0:c0 + LANES], gq_ref[...], ones_ref[...])
    for c0 in range(0, dk, LANES):
        k_ref[:, c0:c0 + LANES] = _norm64_lanes(qkv[:, dq + c0:dq + c0 + LANES], gk_ref[...], ones_ref[...])
    v_ref[...] = qkv[:, dq + dk:]


def _diff_proj_prompt_kernel(x_ref, gmix_ref, wkv_ref, wqvt_ref, gqcol_ref, gk_ref, ones_ref,
                             k_ref, v_ref, kb_ref, qt_ref, vt_ref, *, dq, t):
    h = _rms(x_ref[...], gmix_ref[...]).astype(BF16)
    kv = _dot(h, wkv_ref[...])
    dk = k_ref.shape[1]
    for c0 in range(0, dk, LANES):
        k = _norm64_lanes(kv[:, c0:c0 + LANES], gk_ref[...], ones_ref[...])
        k_ref[:, c0:c0 + LANES] = k
        kb_ref[:, c0:c0 + LANES] = k.astype(BF16)
    v_ref[...] = kv[:, dk:]
    qv = _dot_nt(wqvt_ref[...], h)
    tm = qv.shape[1]
    zq = qv[:dq].reshape(dq // DH, DH, tm)
    ms = jnp.mean(zq * zq, axis=1, keepdims=True)
    qn = (zq * lax.rsqrt(ms + EPS) * gqcol_ref[...]).reshape(dq, tm).astype(BF16)
    vb = _with_ones_rows(qv[dq:].astype(BF16))
    for c in range(tm // t):
        qt_ref[c] = qn[:, c * t:(c + 1) * t]
        vt_ref[c] = vb[:, c * t:(c + 1) * t]


def _diff_proj(x, p, prompt, t=None):
    n, dm = x.shape
    tm = _row_tile(n, PROJ_ROWS)
    dq = dm
    dk = DIFF_KV_HEADS * 2 * DH
    row = lambda w: pl.BlockSpec((tm, w), lambda i: (i, 0))
    if prompt:
        assert tm % t == 0
        ins = [x, p["gmix"], p["w_kv"], p["w_qv_t"], p["g_q_col"], p["g_k"], p["ones64"]]
        feat = lambda w: pl.BlockSpec((tm // t, w, t), lambda i: (i, 0, 0))
        return pl.pallas_call(
            functools.partial(_diff_proj_prompt_kernel, dq=dq, t=t),
            grid=(n // tm,),
            in_specs=[row(dm)] + [_full(a.shape) for a in ins[1:]],
            out_specs=[row(dk), row(dk), row(dk), feat(dq), feat(DIFF_KV_HEADS * V_ROWS)],
            out_shape=[jax.ShapeDtypeStruct((n, dk), F32), jax.ShapeDtypeStruct((n, dk), F32),
                       jax.ShapeDtypeStruct((n, dk), BF16), jax.ShapeDtypeStruct((n // t, dq, t), BF16),
                       jax.ShapeDtypeStruct((n // t, DIFF_KV_HEADS * V_ROWS, t), BF16)],
            compiler_params=_cparams(("parallel",)),
            name="diff_proj_prompt",
        )(*ins)
    ins = [x, p["gmix"], p["w_qkv"], p["g_q"], p["g_k"], p["ones64"]]
    return pl.pallas_call(
        functools.partial(_diff_proj_sample_kernel, dq=dq),
        grid=(n // tm,),
        in_specs=[row(dm)] + [_full(a.shape) for a in ins[1:]],
        out_specs=[row(dq), row(dk), row(dk)],
        out_shape=[jax.ShapeDtypeStruct((n, dq), F32), jax.ShapeDtypeStruct((n, dk), F32),
                   jax.ShapeDtypeStruct((n, dk), F32)],
        compiler_params=_cparams(("parallel",)),
        name="diff_proj_sample",
    )(*ins)


def _diff_attn_kernel(lam_ref, qt_ref, k_ref, vt_ref, bias_ref, gsub_ref, o_ref, qs_ref, m_ref, acc_ref, *, t):
    i = pl.program_id(2)
    group = qt_ref.shape[0] // LANES
    _init_stats(m_ref, None, acc_ref)
    feat = lax.broadcasted_iota(jnp.int32, (LANES, t), 0)
    zero = jnp.zeros((LANES, t), BF16)
    for g in range(group):
        qg = qt_ref[g * LANES:(g + 1) * LANES, :]
        qs_ref[:, (2 * g) * t:(2 * g + 1) * t] = jnp.where(feat < DH, qg, zero)
        qs_ref[:, (2 * g + 1) * t:(2 * g + 2) * t] = jnp.where(feat >= DH, qg, zero)

    def step(j, near):
        off = pl.multiple_of(j * t, t)
        st = _dot(k_ref[pl.ds(off, t), :], qs_ref[...])
        if near is not None:
            st = st + bias_ref[near]
        _softmax_step_t(st, vt_ref[j], m_ref, acc_ref)

    def far(j, carry):
        step(j, None)
        return carry

    lax.fori_loop(0, i - 1, far, 0)

    @pl.when(i >= 1)
    def _():
        step(i - 1, 1)

    step(i, 0)
    acc = acc_ref[0:V_DIM, :] / acc_ref[V_DIM:V_DIM + 1, :]
    lam = lam_ref[0]
    for g in range(group):
        o = acc[:, (2 * g) * t:(2 * g + 1) * t] - lam * acc[:, (2 * g + 1) * t:(2 * g + 2) * t]
        o = o * lax.rsqrt(jnp.mean(o * o, axis=0, keepdims=True) + EPS) * gsub_ref[...]
        o_ref[:, g * LANES:(g + 1) * LANES] = o.T.astype(BF16)


def _diff_prompt_attn(qt, k, vt, bias_t, lam, gsub_col, batch, seq, t):
    nq = seq // t
    heads = qt.shape[1] // LANES
    group = heads // DIFF_KV_HEADS
    gw = group * LANES
    r = group * 2 * t
    return pl.pallas_call(
        functools.partial(_diff_attn_kernel, t=t),
        grid=(batch, DIFF_KV_HEADS, nq),
        in_specs=[pl.BlockSpec(memory_space=pltpu.SMEM),
                  pl.BlockSpec((None, gw, t), lambda b, n, i: (b * nq + i, n, 0)),
                  pl.BlockSpec((seq, LANES), lambda b, n, i: (b, n)),
                  pl.BlockSpec((nq, V_ROWS, t), lambda b, n, i: (b, n, 0)),
                  pl.BlockSpec((2, None, t, r), lambda b, n, i: (0, n, 0, 0)),
                  pl.BlockSpec((LANES, 1), lambda b, n, i: (0, 0))],
        out_specs=pl.BlockSpec((t, gw), lambda b, n, i: (b * nq + i, n)),
        out_shape=jax.ShapeDtypeStruct((batch * seq, heads * LANES), BF16),
        scratch_shapes=[pltpu.VMEM((LANES, r), BF16), pltpu.VMEM((1, r), F32), pltpu.VMEM((V_ROWS, r), F32)],
        compiler_params=_cparams(("parallel", "parallel", "arbitrary")),
        name="diff_prompt_attn",
    )(lam, qt, k, vt, bias_t, gsub_col)


def _stack_diff_queries(q, qs_ref, t):
    lane = lax.broadcasted_iota(jnp.int32, (t, LANES), 1)
    zero = jnp.zeros((t, LANES), q.dtype)
    for g in range(q.shape[1] // LANES):
        qg = q[:, g * LANES:(g + 1) * LANES]
        qs_ref[(2 * g) * t:(2 * g + 1) * t, :] = jnp.where(lane < DH, qg, zero)
        qs_ref[(2 * g + 1) * t:(2 * g + 2) * t, :] = jnp.where(lane >= DH, qg, zero)


def _diff_sample_kernel(pt_ref, lam_ref, q_ref, *rest, pages):
    k_pages = rest[:pages]
    v_pages = rest[pages:2 * pages]
    (knew_ref, vnew_ref, btail_ref, gsub_ref, o_ref,
     qs_ref, k_ref, v_ref, m_ref, l_ref, acc_ref) = rest[2 * pages:]
    j = pl.program_id(1)
    last = pl.num_programs(1) - 1
    t_new = q_ref.shape[0]
    gw = q_ref.shape[1] // DIFF_KV_HEADS
    group = gw // LANES
    rows_n = group * 2 * t_new

    @pl.when(j == 0)
    def _():
        _init_stats(m_ref, l_ref, acc_ref)
        for n in range(DIFF_KV_HEADS):
            _stack_diff_queries(q_ref[:, n * gw:(n + 1) * gw], qs_ref.at[n], t_new)

    def attend(n, lo, hi, bias, mask):
        s = _dot_nt(qs_ref[n].astype(BF16), k_ref[n, lo:hi, :])
        if bias is not None:
            s = s + bias
        if mask is not None:
            s = jnp.where(mask, s, -jnp.inf)
        _softmax_step(s, v_ref[n, lo:hi, :], m_ref.at[n], l_ref.at[n], acc_ref.at[n])

    for pg in range(pages):
        for n in range(DIFF_KV_HEADS):
            head_rows = pl.ds(n, PAGE_SIZE, stride=DIFF_KV_HEADS)
            k_ref[n, pg * PAGE_SIZE:(pg + 1) * PAGE_SIZE, :] = k_pages[pg][head_rows, :].astype(BF16)
            v_ref[n, pg * PAGE_SIZE:(pg + 1) * PAGE_SIZE, :] = v_pages[pg][head_rows, :].astype(BF16)
    n_tok = pages * PAGE_SIZE

    @pl.when(j != last)
    def _():
        for n in range(DIFF_KV_HEADS):
            attend(n, 0, n_tok, None, None)

    @pl.when(j == last)
    def _():
        ca = n_tok - PAGE_SIZE
        tq = lax.broadcasted_iota(jnp.int32, (rows_n, LANES), 0) % t_new
        tk = lax.broadcasted_iota(jnp.int32, (rows_n, LANES), 1)
        lam = lam_ref[0]
        for n in range(DIFF_KV_HEADS):
            if ca > 0:
                attend(n, 0, ca, None, None)
            attend(n, ca, n_tok, btail_ref[n, :, :PAGE_SIZE], None)
            k_ref[n, 0:LANES, :] = _pad_rows(knew_ref[:, n * LANES:(n + 1) * LANES], LANES).astype(BF16)
            v_ref[n, 0:LANES, :] = _pad_rows(vnew_ref[:, n * LANES:(n + 1) * LANES], LANES).astype(BF16)
            attend(n, 0, LANES, btail_ref[n, :, PAGE_SIZE:], tk <= tq)
            acc = acc_ref[n] / l_ref[n]
            for g in range(group):
                a = acc[2 * g * t_new:(2 * g + 2) * t_new]
                o = a[:t_new] - lam * a[t_new:]
                hd = n * group + g
                o_ref[:, hd * LANES:(hd + 1) * LANES] = _rms(o, gsub_ref[...])


def _diff_sample_attn(q, cache_k, cache_v, layer, k_new, v_new, page_table, bias_tail, lam, gsub):
    b, t_new, dq = q.shape
    n_pages = page_table.shape[1]
    pages, steps = _sample_steps(n_pages, DIFF_PAGES_PER_STEP)
    kvw = DIFF_KV_HEADS * LANES
    rows_n = 2 * (dq // LANES // DIFF_KV_HEADS) * t_new
    per_seq = lambda w: pl.BlockSpec((None, t_new, w), lambda i, j, pt: (i, 0, 0))

    def page_spec(pg):
        return pl.BlockSpec((None, None, PAGE_SIZE * DIFF_KV_HEADS, LANES),
                            lambda i, j, pt: (layer, pt[i * n_pages + j * pages + pg], 0, 0))

    in_specs = ([pl.BlockSpec(memory_space=pltpu.SMEM), per_seq(dq)]
                + [page_spec(pg) for pg in range(pages)] * 2
                + [per_seq(kvw), per_seq(kvw),
                   pl.BlockSpec(bias_tail.shape, lambda i, j, pt: (0, 0, 0)),
                   pl.BlockSpec((1, LANES), lambda i, j, pt: (0, 0))])
    per_head = lambda r, w, dt: pltpu.VMEM((DIFF_KV_HEADS, r, w), dt)
    grid_spec = pltpu.PrefetchScalarGridSpec(
        num_scalar_prefetch=1, grid=(b, steps), in_specs=in_specs, out_specs=per_seq(dq),
        scratch_shapes=[per_head(rows_n, LANES, F32),
                        per_head(pages * PAGE_SIZE, LANES, BF16), per_head(pages * PAGE_SIZE, LANES, BF16),
                        per_head(rows_n, 1, F32), per_head(rows_n, 1, F32), per_head(rows_n, LANES, F32)])
    return pl.pallas_call(
        functools.partial(_diff_sample_kernel, pages=pages),
        grid_spec=grid_spec,
        out_shape=jax.ShapeDtypeStruct((b, t_new, dq), F32),
        compiler_params=_cparams(("parallel", "arbitrary")),
        name="diff_sample_attn",
    )(page_table.reshape(-1), lam, q, *([cache_k] * pages), *([cache_v] * pages), k_new, v_new, bias_tail, gsub)


def _post_kernel(x_ref, o_ref, wo_ref, g_ref, win_ref, wout_ref, y_ref, *, hidden, th):
    x1 = x_ref[...] + _dot(o_ref[...].astype(BF16), wo_ref[...])
    h = _rms(x1, g_ref[...]).astype(BF16)
    acc = x1
    for c0 in range(0, hidden, th):
        gate = _dot(h, win_ref[:, c0:c0 + th])
        up = _dot(h, win_ref[:, hidden + c0:hidden + c0 + th])
        act = (gate * jax.nn.sigmoid(gate) * up).astype(BF16)
        acc = acc + _dot(act, wout_ref[c0:c0 + th, :])
    y_ref[...] = acc


def _post(x, o, w_o, g, w_in, w_out):
    n, dm = x.shape
    hidden = w_out.shape[0]
    tm = _row_tile(n, PROJ_ROWS)
    th = 256 if hidden % 256 == 0 else LANES
    assert hidden % th == 0
    row = lambda w: pl.BlockSpec((tm, w), lambda i: (i, 0))
    resident = lambda a: pl.BlockSpec(a.shape, lambda i: (0, 0), pipeline_mode=pl.Buffered(1))
    return pl.pallas_call(
        functools.partial(_post_kernel, hidden=hidden, th=th),
        grid=(n // tm,),
        in_specs=[row(dm), row(o.shape[1]), resident(w_o), _full(g.shape), resident(w_in), resident(w_out)],
        out_specs=row(dm),
        out_shape=jax.ShapeDtypeStruct((n, dm), F32),
        compiler_params=_cparams(("parallel",)),
        name="out_proj_swiglu",
    )(x, o, w_o, g, w_in, w_out)


def _rot_half(a):
    half = a.shape[-1] // 2
    return jnp.concatenate([a[..., half:], a[..., :half]], axis=-1)


def _pad_lanes(a):
    return jnp.concatenate([a, jnp.zeros(a.shape[:-1] + (LANES - a.shape[-1],), a.dtype)], axis=-1)


def _rope_tables(pos):
    half = ROPE // 2
    inv_freq = ROPE_THETA ** (-jnp.arange(half, dtype=F32) / half)
    ang = pos.astype(F32)[:, None] * inv_freq
    cos, sin = jnp.cos(ang), jnp.sin(ang)
    return _pad_lanes(jnp.concatenate([cos, cos], -1)), _pad_lanes(jnp.concatenate([-sin, sin], -1))


def _mla_params(gmix, w_in, g_qa, g_kva, w_qb, w_kvb, g_qn, g_qr, g_kn, g_kr):
    row = lambda a: a.astype(F32).reshape(1, -1)
    w_r = w_in[:, Q_LORA + KV_LORA:]
    w_in_x = jnp.concatenate([w_in[:, :Q_LORA + KV_LORA], _pad_lanes(w_r), _pad_lanes(_rot_half(w_r))], axis=1)
    wq = w_qb.reshape(Q_LORA, MLA_HEADS, NOPE + ROPE)
    wq_r = wq[:, :, NOPE:]
    flat = lambda a: a.reshape(Q_LORA, MLA_HEADS * LANES)
    w_qb_x = jnp.concatenate([flat(wq[:, :, :NOPE]), flat(_pad_lanes(wq_r)), flat(_pad_lanes(_rot_half(wq_r)))], axis=1)
    wkv = w_kvb.reshape(KV_LORA, MLA_HEADS, NOPE + V_DIM)
    w_k = wkv[:, :, :NOPE].reshape(KV_LORA, MLA_HEADS * NOPE)
    w_v = wkv[:, :, NOPE:].reshape(KV_LORA, MLA_HEADS * V_DIM)
    return {
        "gmix": row(gmix), "w_in": w_in_x.astype(BF16), "g_qa": row(g_qa), "g_kva": row(g_kva),
        "g_kr": row(_pad_lanes(g_kr)), "g_kr_rot": row(_pad_lanes(_rot_half(g_kr))),
        "w_qb": w_qb_x.astype(BF16), "w_qb_t": w_qb_x.T.astype(BF16),
        "g_qn": row(g_qn) * MLA_SCALE, "g_qn_col": (g_qn.astype(F32) * (MLA_SCALE * LOG2E)).reshape(-1, 1),
        "g_qr": row(_pad_lanes(g_qr)) * MLA_SCALE, "g_qr_rot": row(_pad_lanes(_rot_half(g_qr))) * MLA_SCALE,
        "g_kn": row(g_kn), "w_k": w_k.astype(BF16), "w_k_t": w_k.T.astype(BF16),
        "w_v": w_v.astype(BF16), "w_v_t": w_v.T.astype(BF16),
    }


def _relative_bias(table, n_dist):
    n = jnp.arange(n_dist, dtype=jnp.int32)
    max_exact = N_BUCKETS // 2
    nf = jnp.maximum(n, 1).astype(F32)
    large = max_exact + (jnp.log(nf / max_exact) / math.log(MAX_DISTANCE / max_exact)
                         * (N_BUCKETS - max_exact)).astype(jnp.int32)
    large = jnp.minimum(large, N_BUCKETS - 1)
    bucket = jnp.where(n < max_exact, n, large)
    tab = table.astype(F32)
    return tab[bucket] - tab[N_BUCKETS - 1]


def _bias_tiles_t(rb, t, group):
    heads = rb.shape[1]
    length = 3 * t - 1
    f = jnp.concatenate([jnp.full((t - 1, heads), -jnp.inf, F32), rb[:2 * t]], axis=0).T
    tiles = []
    for d in range(2):
        w = jnp.roll(f, -(d * t + t - 1), axis=1)
        z = jnp.tile(w, (1, t))[:, :t * (length - 1)].reshape(heads, t, length - 1)[:, :, :t]
        tiles.append(z)
    z = jnp.stack(tiles).reshape(2, DIFF_KV_HEADS, group, 1, t, t)
    z = jnp.broadcast_to(z, (2, DIFF_KV_HEADS, group, 2, t, t))
    return jnp.moveaxis(z, 4, 2).reshape(2, DIFF_KV_HEADS, t, group * 2 * t)


def kernel(x_prompt, x_sample, cache_mla_ckv, cache_mla_krope, cache_diff_k, cache_diff_v, page_table, rel_bias, g_mix_norm, g_ffn_norm, mla_w_in, mla_g_q_a, mla_g_kv_a, mla_w_q_b, mla_w_kv_b, mla_g_qn, mla_g_qr, mla_g_kn, mla_g_kr, mla_w_o, diff_w_qkv, diff_g_qn, diff_g_kn, diff_lq1, diff_lk1, diff_lq2, diff_lk2, diff_g_sub, diff_w_o, ffn_w_in, ffn_w_out):
    batch, seq, dm = x_prompt.shape
    dec_b, dec_t, _ = x_sample.shape
    depth = g_mix_norm.shape[0]
    n_pages = page_table.shape[1]
    past = n_pages * PAGE_SIZE
    heads = dm // (2 * DH)
    group = heads // DIFF_KV_HEADS
    dk = DIFF_KV_HEADS * 2 * DH
    assert past >= MAX_DISTANCE

    xp = x_prompt.reshape(batch * seq, dm)
    xs = x_sample.reshape(dec_b * dec_t, dm)
    row = lambda a: a.astype(F32).reshape(1, -1)

    t_mla = _row_tile(seq, PROJ_ROWS)
    cos_p, sin_p = _rope_tables(jnp.arange(seq, dtype=jnp.int32))
    ts = _row_tile(dec_b * dec_t, PROJ_ROWS)
    cos_s, sin_s = (jnp.tile(a, (ts // dec_t, 1)) for a in _rope_tables(past + jnp.arange(dec_t, dtype=jnp.int32)))

    t_diff = _row_tile(seq, DIFF_TILE)
    assert t_diff >= MAX_DISTANCE
    rb = _relative_bias(rel_bias, max(2 * t_diff, PAGE_SIZE + dec_t))
    bias_t = _bias_tiles_t(rb, t_diff, group) * LOG2E
    tq = jnp.arange(dec_t)
    d_page = PAGE_SIZE + tq[:, None] - jnp.arange(PAGE_SIZE)[None, :]
    d_new = jnp.maximum(tq[:, None] - jnp.arange(LANES)[None, :], 0)
    win = jnp.moveaxis(rb[jnp.concatenate([d_page, d_new], axis=1)], -1, 0)
    win = win.reshape(DIFF_KV_HEADS, group, 1, dec_t, 2 * LANES)
    bias_tail = jnp.broadcast_to(win, (DIFF_KV_HEADS, group, 2, dec_t, 2 * LANES))
    bias_tail = bias_tail.reshape(DIFF_KV_HEADS, -1, 2 * LANES)

    ones64 = jnp.kron(jnp.eye(LANES // DH, dtype=F32), jnp.ones((DH, DH), F32)).astype(BF16)
    ckdk = cache_diff_k.reshape(cache_diff_k.shape[:2] + (PAGE_SIZE * DIFF_KV_HEADS, 2 * DH))
    ckdv = cache_diff_v.reshape(cache_diff_v.shape[:2] + (PAGE_SIZE * DIFF_KV_HEADS, 2 * DH))
    ckrt = jnp.swapaxes(cache_mla_krope, 2, 3)

    mla_out = [[] for _ in range(4)]
    diff_out = [[] for _ in range(4)]
    shp = lambda a: a.reshape(dec_b, dec_t, a.shape[-1])
    for i in range(depth):
        j = i // 2
        w_ffn_in = ffn_w_in[i].astype(BF16)
        w_ffn_out = ffn_w_out[i].astype(BF16)
        g_ffn = row(g_ffn_norm[i])
        if i % 2 == 0:
            p = _mla_params(g_mix_norm[i], mla_w_in[j], mla_g_q_a[j], mla_g_kv_a[j], mla_w_q_b[j], mla_w_kv_b[j],
                            mla_g_qn[j], mla_g_qr[j], mla_g_kn[j], mla_g_kr[j])
            w_o = mla_w_o[j].astype(BF16)
            cgq = (cos_p * p["g_qr"]).T * LOG2E
            sgq = (sin_p * p["g_qr_rot"]).T * LOG2E
            c_p, r_p, qt, k, vt = _mla_proj(xp, cos_p, sin_p, p, True, cgq, sgq)
            op = _mla_prompt_attn(qt, k, vt, batch, seq, t_mla)
            c_s, r_s, qa, qr = _mla_proj(xs, cos_s, sin_s, p, False)
            os_ = _mla_sample_attn(shp(qa), shp(qr), cache_mla_ckv, ckrt, j, shp(c_s), shp(r_s), page_table, p)
            os_ = os_.reshape(dec_b * dec_t, -1)
            for lst, a in zip(mla_out, (c_p.reshape(batch, seq // PAGE_SIZE, PAGE_SIZE, KV_LORA),
                                        r_p.reshape(batch, seq // PAGE_SIZE, PAGE_SIZE, ROPE),
                                        shp(c_s), shp(r_s))):
                lst.append(a)
        else:
            lam_init = 0.8 - 0.6 * math.exp(-0.3 * i)
            lam = (jnp.exp(jnp.sum(diff_lq1[j].astype(F32) * diff_lk1[j].astype(F32)))
                   - jnp.exp(jnp.sum(diff_lq2[j].astype(F32) * diff_lk2[j].astype(F32))) + lam_init).reshape(1)
            gsub = row(diff_g_sub[j]) * (1.0 - lam_init)
            w = diff_w_qkv[j]
            p = {"gmix": row(g_mix_norm[i]), "w_qkv": w.astype(BF16), "w_kv": w[:, dm:].astype(BF16),
                 "w_qv_t": jnp.concatenate([w[:, :dm], w[:, dm + dk:]], axis=1).T.astype(BF16),
                 "g_q": jnp.tile(row(diff_g_qn[j]), (1, LANES // DH)) * DIFF_SCALE,
                 "g_q_col": (diff_g_qn[j].astype(F32) * (DIFF_SCALE * LOG2E)).reshape(-1, 1),
                 "g_k": jnp.tile(row(diff_g_kn[j]), (1, LANES // DH)), "ones64": ones64}
            w_o = diff_w_o[j].astype(BF16)
            k_p, v_p, kb, qt, vt = _diff_proj(xp, p, True, t_diff)
            op = _diff_prompt_attn(qt, kb, vt, bias_t, lam, gsub.reshape(-1, 1), batch, seq, t_diff)
            q_s, k_s, v_s = _diff_proj(xs, p, False)
            os_ = _diff_sample_attn(shp(q_s), ckdk, ckdv, j, shp(k_s), shp(v_s), page_table, bias_tail, lam, gsub)
            os_ = os_.reshape(dec_b * dec_t, -1)
            pg = lambda a: a.reshape(batch, seq // PAGE_SIZE, PAGE_SIZE, DIFF_KV_HEADS, 2 * DH)
            sm = lambda a: a.reshape(dec_b, dec_t, DIFF_KV_HEADS, 2 * DH)
            for lst, a in zip(diff_out, (pg(k_p), pg(v_p), sm(k_s), sm(v_s))):
                lst.append(a)
        xp = _post(xp, op, w_o, g_ffn, w_ffn_in, w_ffn_out)
        xs = _post(xs, os_, w_o, g_ffn, w_ffn_in, w_ffn_out)
    return (xp.reshape(batch, seq, dm), xs.reshape(dec_b, dec_t, dm),
            *(jnp.stack(l) for l in mla_out), *(jnp.stack(l) for l in diff_out))
```

```python
import functools
import math

import jax
import jax.numpy as jnp
from jax import lax
from jax.experimental import pallas as pl
from jax.experimental.pallas import tpu as pltpu

F32 = jnp.float32
BF16 = jnp.bfloat16

EPS = 1e-6
PAGE_SIZE = 128
MLA_HEADS = 8
Q_LORA = 384
KV_LORA = 256
NOPE = 128
ROPE = 64
V_DIM = 128
ROPE_THETA = 10000.0
MLA_SCALE = (NOPE + ROPE) ** -0.5
DH = 64
DIFF_KV_HEADS = 2
DIFF_SCALE = DH ** -0.5
N_BUCKETS = 32
MAX_DISTANCE = 128

LANES = 128
VMEM_LIMIT = 56 * 1024 * 1024
PROJ_ROWS = 512
DIFF_TILE = 256
V_ROWS = V_DIM + 16
MLA_PAGES_PER_STEP = 32
DIFF_PAGES_PER_STEP = 32
NORM_CHUNK = 512
LOG2E = math.log2(math.e)

NT_DIMS = (((1,), (1,)), ((), ()))


def _cparams(sem):
    return pltpu.CompilerParams(dimension_semantics=sem, vmem_limit_bytes=VMEM_LIMIT)


def _rms(x, g):
    return x * lax.rsqrt(jnp.mean(x * x, axis=-1, keepdims=True) + EPS) * g


def _dot(a, b):
    return jnp.dot(a, b, preferred_element_type=F32)


def _dot_nt(a, b):
    return lax.dot_general(a, b, NT_DIMS, preferred_element_type=F32)


def _pad_rows(a, rows):
    return jnp.concatenate([a, jnp.zeros((rows - a.shape[0], a.shape[1]), a.dtype)], axis=0)


def _full(shape):
    return pl.BlockSpec(shape, lambda *_: (0,) * len(shape))


def _row_tile(n, cap):
    t = min(cap, n)
    assert n % t == 0 and t % 8 == 0
    return t


def _mla_proj_kernel(x_ref, cos_ref, sin_ref, gmix_ref, win_ref, gqa_ref, gkva_ref, gkr_ref, gkrr_ref,
                     wqb_ref, gqn_ref, gkn_ref, wk_ref, *rest, prompt):
    if prompt:
        cgq_ref, sgq_ref, wvt_ref, ckv_ref, kr_ref, qt_ref, k_ref, vt_ref = rest
    else:
        gqr_ref, gqrr_ref, ckv_ref, kr_ref, qa_ref, qr_ref = rest
    cos2, sin2 = cos_ref[...], sin_ref[...]
    h = _rms(x_ref[...], gmix_ref[...]).astype(BF16)
    d = _dot(h, win_ref[...])
    cq = _rms(d[:, :Q_LORA], gqa_ref[...]).astype(BF16)
    ckv = _rms(d[:, Q_LORA:Q_LORA + KV_LORA], gkva_ref[...])
    ckv_ref[...] = ckv
    zr = d[:, Q_LORA + KV_LORA:Q_LORA + KV_LORA + LANES]
    zrr = d[:, Q_LORA + KV_LORA + LANES:]
    inv = lax.rsqrt(jnp.sum(zr * zr, axis=-1, keepdims=True) * (1.0 / ROPE) + EPS)
    kr = (zr * gkr_ref[...] * cos2 + zrr * gkrr_ref[...] * sin2) * inv
    kr_ref[...] = kr[:, :ROPE]
    hw = MLA_HEADS * LANES
    cb = ckv.astype(BF16)

    if prompt:
        qall = _dot_nt(wqb_ref[...], cq)
        vt_ref[...] = _with_ones_rows(_dot_nt(wvt_ref[...], cb).astype(BF16))
        kn_all = _dot(cb, wk_ref[...])
        krb = kr.astype(BF16)
        cgq, sgq = cgq_ref[...], sgq_ref[...]
        for hd in range(MLA_HEADS):
            c0 = hd * LANES
            zn = qall[c0:c0 + LANES]
            qn = zn * lax.rsqrt(jnp.mean(zn * zn, axis=0, keepdims=True) + EPS) * gqn_ref[...]
            zr = qall[hw + c0:hw + c0 + LANES]
            zrr = qall[2 * hw + c0:2 * hw + c0 + LANES]
            inv = lax.rsqrt(jnp.sum(zr * zr, axis=0, keepdims=True) * (1.0 / ROPE) + EPS)
            qt_ref[2 * c0:2 * c0 + LANES, :] = qn.astype(BF16)
            qt_ref[2 * c0 + LANES:2 * c0 + 2 * LANES, :] = ((zr * cgq + zrr * sgq) * inv).astype(BF16)
            kn = _rms(kn_all[:, c0:c0 + LANES], gkn_ref[...])
            k_ref[:, 2 * c0:2 * c0 + LANES] = kn.astype(BF16)
            k_ref[:, 2 * c0 + LANES:2 * c0 + 2 * LANES] = krb
    else:
        qall = _dot(cq, wqb_ref[...])
        for hd in range(MLA_HEADS):
            c0 = hd * LANES
            qn = _rms(qall[:, c0:c0 + LANES], gqn_ref[...])
            zr = qall[:, hw + c0:hw + c0 + LANES]
            zrr = qall[:, 2 * hw + c0:2 * hw + c0 + LANES]
            inv = lax.rsqrt(jnp.sum(zr * zr, axis=-1, keepdims=True) * (1.0 / ROPE) + EPS)
            qr_ref[:, c0:c0 + LANES] = (zr * gqr_ref[...] * cos2 + zrr * gqrr_ref[...] * sin2) * inv
            qg = (qn * gkn_ref[...]).astype(BF16)
            qa_ref[:, hd * KV_LORA:(hd + 1) * KV_LORA] = _dot(qg, wk_ref[c0:c0 + LANES, :])


def _mla_proj(x, cos2, sin2, p, prompt, cgq=None, sgq=None):
    n, dm = x.shape
    tm = _row_tile(n, PROJ_ROWS)
    nt = cos2.shape[0] // tm
    hw = MLA_HEADS * LANES
    row = lambda w: pl.BlockSpec((tm, w), lambda i: (i, 0))
    tab = pl.BlockSpec((tm, LANES), lambda i: (i % nt, 0))
    common = [p["gmix"], p["w_in"], p["g_qa"], p["g_kva"], p["g_kr"], p["g_kr_rot"]]
    if prompt:
        tab_t = pl.BlockSpec((LANES, tm), lambda i: (0, i % nt))
        tail = [p["w_qb_t"], p["g_qn_col"], p["g_kn"], p["w_k"]]
        ins = [x, cos2, sin2] + common + tail + [cgq, sgq, p["w_v_t"]]
        specs = ([row(dm), tab, tab] + [_full(a.shape) for a in common + tail]
                 + [tab_t, tab_t, _full(p["w_v_t"].shape)])
        feat = lambda w: pl.BlockSpec((None, w, tm), lambda i: (i, 0, 0))
        out_shape = [jax.ShapeDtypeStruct((n, KV_LORA), F32), jax.ShapeDtypeStruct((n, ROPE), F32),
                     jax.ShapeDtypeStruct((n // tm, 2 * hw, tm), BF16), jax.ShapeDtypeStruct((n, 2 * hw), BF16),
                     jax.ShapeDtypeStruct((n // tm, MLA_HEADS * V_ROWS, tm), BF16)]
        out_specs = [row(KV_LORA), row(ROPE), feat(2 * hw), row(2 * hw), feat(MLA_HEADS * V_ROWS)]
    else:
        tail = [p["w_qb"], p["g_qn"], p["g_kn"], p["w_k_t"], p["g_qr"], p["g_qr_rot"]]
        ins = [x, cos2, sin2] + common + tail
        specs = [row(dm), tab, tab] + [_full(a.shape) for a in common + tail]
        out_shape = [jax.ShapeDtypeStruct((n, KV_LORA), F32), jax.ShapeDtypeStruct((n, ROPE), F32),
                     jax.ShapeDtypeStruct((n, MLA_HEADS * KV_LORA), F32), jax.ShapeDtypeStruct((n, hw), F32)]
        out_specs = [row(KV_LORA), row(ROPE), row(MLA_HEADS * KV_LORA), row(hw)]
    return pl.pallas_call(
        functools.partial(_mla_proj_kernel, prompt=prompt),
        grid=(n // tm,), in_specs=specs, out_specs=out_specs, out_shape=out_shape,
        compiler_params=_cparams(("parallel",)),
        name="mla_proj_prompt" if prompt else "mla_proj_sample",
    )(*ins)


def _softmax_step_t(st, vt, m_ref, acc_ref):
    m_old = m_ref[...]
    m_new = jnp.maximum(m_old, jnp.max(st, axis=0, keepdims=True))
    p = jnp.exp2(st - m_new).astype(BF16)
    acc_ref[...] = jnp.exp2(m_old - m_new) * acc_ref[...] + _dot(vt, p)
    m_ref[...] = m_new


def _with_ones_rows(vt_heads):
    n = vt_heads.shape[1]
    ones = jnp.ones((V_ROWS - V_DIM, n), BF16)
    parts = []
    for h0 in range(0, vt_heads.shape[0], V_DIM):
        parts += [vt_heads[h0:h0 + V_DIM], ones]
    return jnp.concatenate(parts, axis=0)


def _init_stats(m_ref, l_ref, acc_ref):
    m_ref[...] = jnp.full(m_ref.shape, -jnp.inf, F32)
    if l_ref is not None:
        l_ref[...] = jnp.zeros(l_ref.shape, F32)
    acc_ref[...] = jnp.zeros(acc_ref.shape, F32)


def _mla_attn_kernel(qt_ref, k_ref, vt_ref, o_ref, sa_ref, sb_ref, m_ref, acc_ref, *, t):
    i = pl.program_id(2)
    _init_stats(m_ref, None, acc_ref)

    def scores(j, s_ref):
        off = pl.multiple_of(j * t, t)
        s_ref[...] = _dot(k_ref[pl.ds(off, t), :], qt_ref[...])

    def update(j, s_ref, diagonal):
        st = s_ref[...]
        if diagonal:
            keys = lax.broadcasted_iota(jnp.int32, (t, t), 0)
            queries = lax.broadcasted_iota(jnp.int32, (t, t), 1)
            st = jnp.where(keys <= queries, st, -jnp.inf)
        _softmax_step_t(st, vt_ref[j], m_ref, acc_ref)

    scores(0, sa_ref)

    def pair(jj, carry):
        a = 2 * jj
        scores(a + 1, sb_ref)
        update(a, sa_ref, False)
        scores(a + 2, sa_ref)
        update(a + 1, sb_ref, False)
        return carry

    lax.fori_loop(0, i // 2, pair, 0)

    @pl.when(i % 2 == 0)
    def _():
        update(i, sa_ref, True)

    @pl.when(i % 2 == 1)
    def _():
        scores(i, sb_ref)
        update(i - 1, sa_ref, False)
        update(i, sb_ref, True)

    o_ref[...] = (acc_ref[0:V_DIM, :] / acc_ref[V_DIM:V_DIM + 1, :]).T.astype(BF16)


def _mla_prompt_attn(qt, k, vt, batch, seq, t):
    nq = seq // t
    kw = 2 * LANES
    return pl.pallas_call(
        functools.partial(_mla_attn_kernel, t=t),
        grid=(batch, MLA_HEADS, nq),
        in_specs=[pl.BlockSpec((None, kw, t), lambda b, h, i: (b * nq + i, h, 0)),
                  pl.BlockSpec((seq, kw), lambda b, h, i: (b, h)),
                  pl.BlockSpec((nq, V_ROWS, t), lambda b, h, i: (b, h, 0))],
        out_specs=pl.BlockSpec((t, V_DIM), lambda b, h, i: (b * nq + i, h)),
        out_shape=jax.ShapeDtypeStruct((batch * seq, MLA_HEADS * V_DIM), BF16),
        scratch_shapes=[pltpu.VMEM((t, t), F32), pltpu.VMEM((t, t), F32),
                        pltpu.VMEM((1, t), F32), pltpu.VMEM((V_ROWS, t), F32)],
        compiler_params=_cparams(("parallel", "parallel", "arbitrary")),
        name="mla_prompt_attn",
    )(qt, k, vt)


def _softmax_step(s, v, m_ref, l_ref, acc_ref):
    m_old = m_ref[...]
    m_new = jnp.maximum(m_old, jnp.max(s, axis=-1, keepdims=True))
    alpha = jnp.exp(m_old - m_new)
    p = jnp.exp(s - m_new)
    l_ref[...] = alpha * l_ref[...] + jnp.sum(p, axis=-1, keepdims=True)
    acc_ref[...] = alpha * acc_ref[...] + _dot(p.astype(BF16), v)
    m_ref[...] = m_new


def _mla_sample_kernel(pt_ref, qa_ref, qr_ref, *rest, pages, chunk):
    c_pages = rest[:pages]
    r_pages = rest[pages:2 * pages]
    (cnew_ref, rnew_ref, wkt_ref, wv_ref, o_ref,
     wq_ref, qrs_ref, c_ref, r_ref, inv_ref, s_ref, m_ref, l_ref, u_ref) = rest[2 * pages:]
    j = pl.program_id(1)
    t_new = cnew_ref.shape[0]
    rows = MLA_HEADS * t_new
    n_tok = pages * PAGE_SIZE
    kw = MLA_HEADS * NOPE

    @pl.when(j == 0)
    def _():
        _init_stats(m_ref, l_ref, u_ref)
        wq_ref[0:kw, :] = wkt_ref[...]
        wq_ref[kw:, :] = jnp.concatenate(
            [qa_ref[:, hd * KV_LORA:(hd + 1) * KV_LORA] for hd in range(MLA_HEADS)], axis=0).astype(BF16)
        qrs_ref[...] = jnp.concatenate(
            [qr_ref[:, hd * LANES:hd * LANES + ROPE] for hd in range(MLA_HEADS)], axis=0).astype(BF16)

    def expand(lo, hi):
        kt = _dot_nt(wq_ref[...], c_ref[lo:hi, :])
        ssq = jnp.sum((kt[0:kw] * kt[0:kw]).reshape(MLA_HEADS, NOPE, hi - lo), axis=1)
        inv_ref[:, lo:hi] = lax.rsqrt(ssq * (1.0 / NOPE) + EPS)
        s_ref[:, lo:hi] = kt[kw:]

    def attend(n, s_rope, mask):
        s = s_ref[:, 0:n].reshape(MLA_HEADS, t_new, n) * inv_ref[:, 0:n][:, None, :]
        s = s.reshape(rows, n) + s_rope
        if mask is not None:
            s = jnp.where(mask, s, -jnp.inf)
        _softmax_step(s, c_ref[0:n, :], m_ref, l_ref, u_ref)

    for pg in range(pages):
        c_ref[pg * PAGE_SIZE:(pg + 1) * PAGE_SIZE, :] = c_pages[pg][...].astype(BF16)
        r_ref[:, pg * PAGE_SIZE:(pg + 1) * PAGE_SIZE] = r_pages[pg][...].astype(BF16)
    for c0 in range(0, n_tok, chunk):
        expand(c0, c0 + chunk)
    attend(n_tok, _dot(qrs_ref[...], r_ref[...]), None)

    @pl.when(j == pl.num_programs(1) - 1)
    def _():
        c_ref[0:LANES, :] = _pad_rows(cnew_ref[...], LANES).astype(BF16)
        r_new = _pad_rows(rnew_ref[...], LANES).astype(BF16)
        tq = lax.broadcasted_iota(jnp.int32, (rows, LANES), 0) % t_new
        tk = lax.broadcasted_iota(jnp.int32, (rows, LANES), 1)
        expand(0, LANES)
        attend(LANES, _dot_nt(qrs_ref[...], r_new), tk <= tq)
        o_all = _dot((u_ref[...] / l_ref[...]).astype(BF16), wv_ref[...])
        for hd in range(MLA_HEADS):
            o_ref[:, hd * V_DIM:(hd + 1) * V_DIM] = o_all[hd * t_new:(hd + 1) * t_new, hd * V_DIM:(hd + 1) * V_DIM]


def _sample_steps(n_pages, cap):
    pages = min(cap, n_pages)
    assert n_pages % pages == 0
    return pages, n_pages // pages


def _mla_sample_attn(qa, qr, cache_c, cache_rt, layer, c_new, r_new, page_table, p):
    b, t_new, _ = qa.shape
    n_pages = page_table.shape[1]
    pages, steps = _sample_steps(n_pages, MLA_PAGES_PER_STEP)
    chunk = min(NORM_CHUNK, pages * PAGE_SIZE)
    rows = MLA_HEADS * t_new
    per_seq = lambda w: pl.BlockSpec((None, t_new, w), lambda i, j, pt: (i, 0, 0))

    def page_spec(r, w, pg):
        return pl.BlockSpec((None, None, r, w), lambda i, j, pt: (layer, pt[i * n_pages + j * pages + pg], 0, 0))

    in_specs = ([per_seq(MLA_HEADS * KV_LORA), per_seq(MLA_HEADS * LANES)]
                + [page_spec(PAGE_SIZE, KV_LORA, pg) for pg in range(pages)]
                + [page_spec(ROPE, PAGE_SIZE, pg) for pg in range(pages)]
                + [per_seq(KV_LORA), per_seq(ROPE),
                   pl.BlockSpec(p["w_k_t"].shape, lambda i, j, pt: (0, 0)),
                   pl.BlockSpec(p["w_v"].shape, lambda i, j, pt: (0, 0))])
    grid_spec = pltpu.PrefetchScalarGridSpec(
        num_scalar_prefetch=1, grid=(b, steps), in_specs=in_specs,
        out_specs=per_seq(MLA_HEADS * V_DIM),
        scratch_shapes=[pltpu.VMEM((MLA_HEADS * NOPE + rows, KV_LORA), BF16), pltpu.VMEM((rows, ROPE), BF16),
                        pltpu.VMEM((pages * PAGE_SIZE, KV_LORA), BF16), pltpu.VMEM((ROPE, pages * PAGE_SIZE), BF16),
                        pltpu.VMEM((MLA_HEADS, pages * PAGE_SIZE), F32), pltpu.VMEM((rows, pages * PAGE_SIZE), F32),
                        pltpu.VMEM((rows, 1), F32), pltpu.VMEM((rows, 1), F32), pltpu.VMEM((rows, KV_LORA), F32)])
    return pl.pallas_call(
        functools.partial(_mla_sample_kernel, pages=pages, chunk=chunk),
        grid_spec=grid_spec,
        out_shape=jax.ShapeDtypeStruct((b, t_new, MLA_HEADS * V_DIM), F32),
        compiler_params=_cparams(("parallel", "arbitrary")),
        name="mla_sample_attn",
    )(page_table.reshape(-1), qa, qr, *([cache_c] * pages), *([cache_rt] * pages), c_new, r_new,
      p["w_k_t"], p["w_v"])


def _norm64_lanes(z, g, ones):
    ms = _dot((z * z).astype(BF16), ones) * (1.0 / DH)
    return z * lax.rsqrt(ms + EPS) * g


def _diff_proj_sample_kernel(x_ref, gmix_ref, w_ref, gq_ref, gk_ref, ones_ref, q_ref, k_ref, v_ref, *, dq):
    h = _rms(x_ref[...], gmix_ref[...]).astype(BF16)
    qkv = _dot(h, w_ref[...])
    dk = k_ref.shape[1]
    for c0 in range(0, dq, LANES):
        q_ref[:, c0:c0 + LANES] = _norm64_lanes(qkv[:, c0:c0 + LANES], gq_ref[...], ones_ref[...])
    for c0 in range(0, dk, LANES):
        k_ref[:, c0:c0 + LANES] = _norm64_lanes(qkv[:, dq + c0:dq + c0 + LANES], gk_ref[...], ones_ref[...])
    v_ref[...] = qkv[:, dq + dk:]


def _diff_proj_prompt_kernel(x_ref, gmix_ref, wkv_ref, wqvt_ref, gqcol_ref, gk_ref, ones_ref,
                             k_ref, v_ref, kb_ref, qt_ref, vt_ref, *, dq, t):
    h = _rms(x_ref[...], gmix_ref[...]).astype(BF16)
    kv = _dot(h, wkv_ref[...])
    dk = k_ref.shape[1]
    for c0 in range(0, dk, LANES):
        k = _norm64_lanes(kv[:, c0:c0 + LANES], gk_ref[...], ones_ref[...])
        k_ref[:, c0:c0 + LANES] = k
        kb_ref[:, c0:c0 + LANES] = k.astype(BF16)
    v_ref[...] = kv[:, dk:]
    qv = _dot_nt(wqvt_ref[...], h)
    tm = qv.shape[1]
    zq = qv[:dq].reshape(dq // DH, DH, tm)
    ms = jnp.mean(zq * zq, axis=1, keepdims=True)
    qn = (zq * lax.rsqrt(ms + EPS) * gqcol_ref[...]).reshape(dq, tm).astype(BF16)
    vb = _with_ones_rows(qv[dq:].astype(BF16))
    for c in range(tm // t):
        qt_ref[c] = qn[:, c * t:(c + 1) * t]
        vt_ref[c] = vb[:, c * t:(c + 1) * t]


def _diff_proj(x, p, prompt, t=None):
    n, dm = x.shape
    tm = _row_tile(n, PROJ_ROWS)
    dq = dm
    dk = DIFF_KV_HEADS * 2 * DH
    row = lambda w: pl.BlockSpec((tm, w), lambda i: (i, 0))
    if prompt:
        assert tm % t == 0
        ins = [x, p["gmix"], p["w_kv"], p["w_qv_t"], p["g_q_col"], p["g_k"], p["ones64"]]
        feat = lambda w: pl.BlockSpec((tm // t, w, t), lambda i: (i, 0, 0))
        return pl.pallas_call(
            functools.partial(_diff_proj_prompt_kernel, dq=dq, t=t),
            grid=(n // tm,),
            in_specs=[row(dm)] + [_full(a.shape) for a in ins[1:]],
            out_specs=[row(dk), row(dk), row(dk), feat(dq), feat(DIFF_KV_HEADS * V_ROWS)],
            out_shape=[jax.ShapeDtypeStruct((n, dk), F32), jax.ShapeDtypeStruct((n, dk), F32),
                       jax.ShapeDtypeStruct((n, dk), BF16), jax.ShapeDtypeStruct((n // t, dq, t), BF16),
                       jax.ShapeDtypeStruct((n // t, DIFF_KV_HEADS * V_ROWS, t), BF16)],
            compiler_params=_cparams(("parallel",)),
            name="diff_proj_prompt",
        )(*ins)
    ins = [x, p["gmix"], p["w_qkv"], p["g_q"], p["g_k"], p["ones64"]]
    return pl.pallas_call(
        functools.partial(_diff_proj_sample_kernel, dq=dq),
        grid=(n // tm,),
        in_specs=[row(dm)] + [_full(a.shape) for a in ins[1:]],
        out_specs=[row(dq), row(dk), row(dk)],
        out_shape=[jax.ShapeDtypeStruct((n, dq), F32), jax.ShapeDtypeStruct((n, dk), F32),
                   jax.ShapeDtypeStruct((n, dk), F32)],
        compiler_params=_cparams(("parallel",)),
        name="diff_proj_sample",
    )(*ins)


def _diff_attn_kernel(lam_ref, qt_ref, k_ref, vt_ref, bias_ref, gsub_ref, o_ref, qs_ref, sa_ref, sb_ref, m_ref,
                      acc_ref, *, t):
    i = pl.program_id(2)
    group = qt_ref.shape[0] // LANES
    _init_stats(m_ref, None, acc_ref)
    feat = lax.broadcasted_iota(jnp.int32, (LANES, t), 0)
    zero = jnp.zeros((LANES, t), BF16)
    for g in range(group):
        qg = qt_ref[g * LANES:(g + 1) * LANES, :]
        qs_ref[:, (2 * g) * t:(2 * g + 1) * t] = jnp.where(feat < DH, qg, zero)
        qs_ref[:, (2 * g + 1) * t:(2 * g + 2) * t] = jnp.where(feat >= DH, qg, zero)

    def scores(j, s_ref):
        off = pl.multiple_of(j * t, t)
        s_ref[...] = _dot(k_ref[pl.ds(off, t), :], qs_ref[...])

    def update(j, s_ref, near):
        st = s_ref[...]
        if near is not None:
            st = st + bias_ref[near]
        _softmax_step_t(st, vt_ref[j], m_ref, acc_ref)

    n_far = jnp.maximum(i - 1, 0)
    scores(0, sa_ref)

    def pair(jj, carry):
        a = 2 * jj
        scores(a + 1, sb_ref)
        update(a, sa_ref, None)
        scores(a + 2, sa_ref)
        update(a + 1, sb_ref, None)
        return carry

    lax.fori_loop(0, n_far // 2, pair, 0)

    @pl.when(i == 0)
    def _():
        update(0, sa_ref, 0)

    @pl.when((i >= 1) & (n_far % 2 == 0))
    def _():
        scores(i, sb_ref)
        update(i - 1, sa_ref, 1)
        update(i, sb_ref, 0)

    @pl.when(n_far % 2 == 1)
    def _():
        scores(i - 1, sb_ref)
        update(i - 2, sa_ref, None)
        scores(i, sa_ref)
        update(i - 1, sb_ref, 1)
        update(i, sa_ref, 0)

    acc = acc_ref[0:V_DIM, :] / acc_ref[V_DIM:V_DIM + 1, :]
    lam = lam_ref[0]
    for g in range(group):
        o = acc[:, (2 * g) * t:(2 * g + 1) * t] - lam * acc[:, (2 * g + 1) * t:(2 * g + 2) * t]
        o = o * lax.rsqrt(jnp.mean(o * o, axis=0, keepdims=True) + EPS) * gsub_ref[...]
        o_ref[:, g * LANES:(g + 1) * LANES] = o.T.astype(BF16)


def _diff_prompt_attn(qt, k, vt, bias_t, lam, gsub_col, batch, seq, t):
    nq = seq // t
    heads = qt.shape[1] // LANES
    group = heads // DIFF_KV_HEADS
    gw = group * LANES
    r = group * 2 * t
    return pl.pallas_call(
        functools.partial(_diff_attn_kernel, t=t),
        grid=(batch, DIFF_KV_HEADS, nq),
        in_specs=[pl.BlockSpec(memory_space=pltpu.SMEM),
                  pl.BlockSpec((None, gw, t), lambda b, n, i: (b * nq + i, n, 0)),
                  pl.BlockSpec((seq, LANES), lambda b, n, i: (b, n)),
                  pl.BlockSpec((nq, V_ROWS, t), lambda b, n, i: (b, n, 0)),
                  pl.BlockSpec((2, None, t, r), lambda b, n, i: (0, n, 0, 0)),
                  pl.BlockSpec((LANES, 1), lambda b, n, i: (0, 0))],
        out_specs=pl.BlockSpec((t, gw), lambda b, n, i: (b * nq + i, n)),
        out_shape=jax.ShapeDtypeStruct((batch * seq, heads * LANES), BF16),
        scratch_shapes=[pltpu.VMEM((LANES, r), BF16), pltpu.VMEM((t, r), F32), pltpu.VMEM((t, r), F32),
                        pltpu.VMEM((1, r), F32), pltpu.VMEM((V_ROWS, r), F32)],
        compiler_params=_cparams(("parallel", "parallel", "arbitrary")),
        name="diff_prompt_attn",
    )(lam, qt, k, vt, bias_t, gsub_col)


def _stack_diff_queries(q, qs_ref, t):
    lane = lax.broadcasted_iota(jnp.int32, (t, LANES), 1)
    zero = jnp.zeros((t, LANES), q.dtype)
    for g in range(q.shape[1] // LANES):
        qg = q[:, g * LANES:(g + 1) * LANES]
        qs_ref[(2 * g) * t:(2 * g + 1) * t, :] = jnp.where(lane < DH, qg, zero)
        qs_ref[(2 * g + 1) * t:(2 * g + 2) * t, :] = jnp.where(lane >= DH, qg, zero)


def _diff_sample_kernel(pt_ref, lam_ref, q_ref, *rest, pages):
    k_pages = rest[:pages]
    v_pages = rest[pages:2 * pages]
    (knew_ref, vnew_ref, btail_ref, gsub_ref, o_ref,
     qs_ref, k_ref, v_ref, m_ref, l_ref, acc_ref) = rest[2 * pages:]
    j = pl.program_id(1)
    last = pl.num_programs(1) - 1
    t_new = q_ref.shape[0]
    gw = q_ref.shape[1] // DIFF_KV_HEADS
    group = gw // LANES
    rows_n = group * 2 * t_new

    @pl.when(j == 0)
    def _():
        _init_stats(m_ref, l_ref, acc_ref)
        for n in range(DIFF_KV_HEADS):
            _stack_diff_queries(q_ref[:, n * gw:(n + 1) * gw], qs_ref.at[n], t_new)

    def attend(n, lo, hi, bias, mask):
        s = _dot_nt(qs_ref[n].astype(BF16), k_ref[n, lo:hi, :])
        if bias is not None:
            s = s + bias
        if mask is not None:
            s = jnp.where(mask, s, -jnp.inf)
        _softmax_step(s, v_ref[n, lo:hi, :], m_ref.at[n], l_ref.at[n], acc_ref.at[n])

    for pg in range(pages):
        for n in range(DIFF_KV_HEADS):
            head_rows = pl.ds(n, PAGE_SIZE, stride=DIFF_KV_HEADS)
            k_ref[n, pg * PAGE_SIZE:(pg + 1) * PAGE_SIZE, :] = k_pages[pg][head_rows, :].astype(BF16)
            v_ref[n, pg * PAGE_SIZE:(pg + 1) * PAGE_SIZE, :] = v_pages[pg][head_rows, :].astype(BF16)
    n_tok = pages * PAGE_SIZE

    @pl.when(j != last)
    def _():
        for n in range(DIFF_KV_HEADS):
            attend(n, 0, n_tok, None, None)

    @pl.when(j == last)
    def _():
        ca = n_tok - PAGE_SIZE
        tq = lax.broadcasted_iota(jnp.int32, (rows_n, LANES), 0) % t_new
        tk = lax.broadcasted_iota(jnp.int32, (rows_n, LANES), 1)
        lam = lam_ref[0]
        for n in range(DIFF_KV_HEADS):
            if ca > 0:
                attend(n, 0, ca, None, None)
            attend(n, ca, n_tok, btail_ref[n, :, :PAGE_SIZE], None)
            k_ref[n, 0:LANES, :] = _pad_rows(knew_ref[:, n * LANES:(n + 1) * LANES], LANES).astype(BF16)
            v_ref[n, 0:LANES, :] = _pad_rows(vnew_ref[:, n * LANES:(n + 1) * LANES], LANES).astype(BF16)
            attend(n, 0, LANES, btail_ref[n, :, PAGE_SIZE:], tk <= tq)
            acc = acc_ref[n] / l_ref[n]
            for g in range(group):
                a = acc[2 * g * t_new:(2 * g + 2) * t_new]
                o = a[:t_new] - lam * a[t_new:]
                hd = n * group + g
                o_ref[:, hd * LANES:(hd + 1) * LANES] = _rms(o, gsub_ref[...])


def _diff_sample_attn(q, cache_k, cache_v, layer, k_new, v_new, page_table, bias_tail, lam, gsub):
    b, t_new, dq = q.shape
    n_pages = page_table.shape[1]
    pages, steps = _sample_steps(n_pages, DIFF_PAGES_PER_STEP)
    kvw = DIFF_KV_HEADS * LANES
    rows_n = 2 * (dq // LANES // DIFF_KV_HEADS) * t_new
    per_seq = lambda w: pl.BlockSpec((None, t_new, w), lambda i, j, pt: (i, 0, 0))

    def page_spec(pg):
        return pl.BlockSpec((None, None, PAGE_SIZE * DIFF_KV_HEADS, LANES),
                            lambda i, j, pt: (layer, pt[i * n_pages + j * pages + pg], 0, 0))

    in_specs = ([pl.BlockSpec(memory_space=pltpu.SMEM), per_seq(dq)]
                + [page_spec(pg) for pg in range(pages)] * 2
                + [per_seq(kvw), per_seq(kvw),
                   pl.BlockSpec(bias_tail.shape, lambda i, j, pt: (0, 0, 0)),
                   pl.BlockSpec((1, LANES), lambda i, j, pt: (0, 0))])
    per_head = lambda r, w, dt: pltpu.VMEM((DIFF_KV_HEADS, r, w), dt)
    grid_spec = pltpu.PrefetchScalarGridSpec(
        num_scalar_prefetch=1, grid=(b, steps), in_specs=in_specs, out_specs=per_seq(dq),
        scratch_shapes=[per_head(rows_n, LANES, F32),
                        per_head(pages * PAGE_SIZE, LANES, BF16), per_head(pages * PAGE_SIZE, LANES, BF16),
                        per_head(rows_n, 1, F32), per_head(rows_n, 1, F32), per_head(rows_n, LANES, F32)])
    return pl.pallas_call(
        functools.partial(_diff_sample_kernel, pages=pages),
        grid_spec=grid_spec,
        out_shape=jax.ShapeDtypeStruct((b, t_new, dq), F32),
        compiler_params=_cparams(("parallel", "arbitrary")),
        name="diff_sample_attn",
    )(page_table.reshape(-1), lam, q, *([cache_k] * pages), *([cache_v] * pages), k_new, v_new, bias_tail, gsub)


def _post_kernel(x_ref, o_ref, wo_ref, g_ref, win_ref, wout_ref, y_ref, *, hidden, th):
    x1 = x_ref[...] + _dot(o_ref[...].astype(BF16), wo_ref[...])
    h = _rms(x1, g_ref[...]).astype(BF16)
    acc = x1
    for c0 in range(0, hidden, th):
        gate = _dot(h, win_ref[:, c0:c0 + th])
        up = _dot(h, win_ref[:, hidden + c0:hidden + c0 + th])
        act = (gate * jax.nn.sigmoid(gate) * up).astype(BF16)
        acc = acc + _dot(act, wout_ref[c0:c0 + th, :])
    y_ref[...] = acc


def _post(x, o, w_o, g, w_in, w_out):
    n, dm = x.shape
    hidden = w_out.shape[0]
    tm = _row_tile(n, PROJ_ROWS)
    th = 256 if hidden % 256 == 0 else LANES
    assert hidden % th == 0
    row = lambda w: pl.BlockSpec((tm, w), lambda i: (i, 0))
    resident = lambda a: pl.BlockSpec(a.shape, lambda i: (0, 0), pipeline_mode=pl.Buffered(1))
    return pl.pallas_call(
        functools.partial(_post_kernel, hidden=hidden, th=th),
        grid=(n // tm,),
        in_specs=[row(dm), row(o.shape[1]), resident(w_o), _full(g.shape), resident(w_in), resident(w_out)],
        out_specs=row(dm),
        out_shape=jax.ShapeDtypeStruct((n, dm), F32),
        compiler_params=_cparams(("parallel",)),
        name="out_proj_swiglu",
    )(x, o, w_o, g, w_in, w_out)


def _rot_half(a):
    half = a.shape[-1] // 2
    return jnp.concatenate([a[..., half:], a[..., :half]], axis=-1)


def _pad_lanes(a):
    return jnp.concatenate([a, jnp.zeros(a.shape[:-1] + (LANES - a.shape[-1],), a.dtype)], axis=-1)


def _rope_tables(pos):
    half = ROPE // 2
    inv_freq = ROPE_THETA ** (-jnp.arange(half, dtype=F32) / half)
    ang = pos.astype(F32)[:, None] * inv_freq
    cos, sin = jnp.cos(ang), jnp.sin(ang)
    return _pad_lanes(jnp.concatenate([cos, cos], -1)), _pad_lanes(jnp.concatenate([-sin, sin], -1))


def _mla_params(gmix, w_in, g_qa, g_kva, w_qb, w_kvb, g_qn, g_qr, g_kn, g_kr):
    row = lambda a: a.astype(F32).reshape(1, -1)
    w_r = w_in[:, Q_LORA + KV_LORA:]
    w_in_x = jnp.concatenate([w_in[:, :Q_LORA + KV_LORA], _pad_lanes(w_r), _pad_lanes(_rot_half(w_r))], axis=1)
    wq = w_qb.reshape(Q_LORA, MLA_HEADS, NOPE + ROPE)
    wq_r = wq[:, :, NOPE:]
    flat = lambda a: a.reshape(Q_LORA, MLA_HEADS * LANES)
    w_qb_x = jnp.concatenate([flat(wq[:, :, :NOPE]), flat(_pad_lanes(wq_r)), flat(_pad_lanes(_rot_half(wq_r)))], axis=1)
    wkv = w_kvb.reshape(KV_LORA, MLA_HEADS, NOPE + V_DIM)
    w_k = wkv[:, :, :NOPE].reshape(KV_LORA, MLA_HEADS * NOPE)
    w_v = wkv[:, :, NOPE:].reshape(KV_LORA, MLA_HEADS * V_DIM)
    return {
        "gmix": row(gmix), "w_in": w_in_x.astype(BF16), "g_qa": row(g_qa), "g_kva": row(g_kva),
        "g_kr": row(_pad_lanes(g_kr)), "g_kr_rot": row(_pad_lanes(_rot_half(g_kr))),
        "w_qb": w_qb_x.astype(BF16), "w_qb_t": w_qb_x.T.astype(BF16),
        "g_qn": row(g_qn) * MLA_SCALE, "g_qn_col": (g_qn.astype(F32) * (MLA_SCALE * LOG2E)).reshape(-1, 1),
        "g_qr": row(_pad_lanes(g_qr)) * MLA_SCALE, "g_qr_rot": row(_pad_lanes(_rot_half(g_qr))) * MLA_SCALE,
        "g_kn": row(g_kn), "w_k": w_k.astype(BF16), "w_k_t": w_k.T.astype(BF16),
        "w_v": w_v.astype(BF16), "w_v_t": w_v.T.astype(BF16),
    }


def _relative_bias(table, n_dist):
    n = jnp.arange(n_dist, dtype=jnp.int32)
    max_exact = N_BUCKETS // 2
    nf = jnp.maximum(n, 1).astype(F32)
    large = max_exact + (jnp.log(nf / max_exact) / math.log(MAX_DISTANCE / max_exact)
                         * (N_BUCKETS - max_exact)).astype(jnp.int32)
    large = jnp.minimum(large, N_BUCKETS - 1)
    bucket = jnp.where(n < max_exact, n, large)
    tab = table.astype(F32)
    return tab[bucket] - tab[N_BUCKETS - 1]


def _bias_tiles_t(rb, t, group):
    heads = rb.shape[1]
    length = 3 * t - 1
    f = jnp.concatenate([jnp.full((t - 1, heads), -jnp.inf, F32), rb[:2 * t]], axis=0).T
    tiles = []
    for d in range(2):
        w = jnp.roll(f, -(d * t + t - 1), axis=1)
        z = jnp.tile(w, (1, t))[:, :t * (length - 1)].reshape(heads, t, length - 1)[:, :, :t]
        tiles.append(z)
    z = jnp.stack(tiles).reshape(2, DIFF_KV_HEADS, group, 1, t, t)
    z = jnp.broadcast_to(z, (2, DIFF_KV_HEADS, group, 2, t, t))
    return jnp.moveaxis(z, 4, 2).reshape(2, DIFF_KV_HEADS, t, group * 2 * t)


def kernel(x_prompt, x_sample, cache_mla_ckv, cache_mla_krope, cache_diff_k, cache_diff_v, page_table, rel_bias, g_mix_norm, g_ffn_norm, mla_w_in, mla_g_q_a, mla_g_kv_a, mla_w_q_b, mla_w_kv_b, mla_g_qn, mla_g_qr, mla_g_kn, mla_g_kr, mla_w_o, diff_w_qkv, diff_g_qn, diff_g_kn, diff_lq1, diff_lk1, diff_lq2, diff_lk2, diff_g_sub, diff_w_o, ffn_w_in, ffn_w_out):
    batch, seq, dm = x_prompt.shape
    dec_b, dec_t, _ = x_sample.shape
    depth = g_mix_norm.shape[0]
    n_pages = page_table.shape[1]
    past = n_pages * PAGE_SIZE
    heads = dm // (2 * DH)
    group = heads // DIFF_KV_HEADS
    dk = DIFF_KV_HEADS * 2 * DH
    assert past >= MAX_DISTANCE

    xp = x_prompt.reshape(batch * seq, dm)
    xs = x_sample.reshape(dec_b * dec_t, dm)
    row = lambda a: a.astype(F32).reshape(1, -1)

    t_mla = _row_tile(seq, PROJ_ROWS)
    cos_p, sin_p = _rope_tables(jnp.arange(seq, dtype=jnp.int32))
    ts = _row_tile(dec_b * dec_t, PROJ_ROWS)
    cos_s, sin_s = (jnp.tile(a, (ts // dec_t, 1)) for a in _rope_tables(past + jnp.arange(dec_t, dtype=jnp.int32)))

    t_diff = _row_tile(seq, DIFF_TILE)
    assert t_diff >= MAX_DISTANCE
    rb = _relative_bias(rel_bias, max(2 * t_diff, PAGE_SIZE + dec_t))
    bias_t = _bias_tiles_t(rb, t_diff, group) * LOG2E
    tq = jnp.arange(dec_t)
    d_page = PAGE_SIZE + tq[:, None] - jnp.arange(PAGE_SIZE)[None, :]
    d_new = jnp.maximum(tq[:, None] - jnp.arange(LANES)[None, :], 0)
    win = jnp.moveaxis(rb[jnp.concatenate([d_page, d_new], axis=1)], -1, 0)
    win = win.reshape(DIFF_KV_HEADS, group, 1, dec_t, 2 * LANES)
    bias_tail = jnp.broadcast_to(win, (DIFF_KV_HEADS, group, 2, dec_t, 2 * LANES))
    bias_tail = bias_tail.reshape(DIFF_KV_HEADS, -1, 2 * LANES)

    ones64 = jnp.kron(jnp.eye(LANES // DH, dtype=F32), jnp.ones((DH, DH), F32)).astype(BF16)
    ckdk = cache_diff_k.reshape(cache_diff_k.shape[:2] + (PAGE_SIZE * DIFF_KV_HEADS, 2 * DH))
    ckdv = cache_diff_v.reshape(cache_diff_v.shape[:2] + (PAGE_SIZE * DIFF_KV_HEADS, 2 * DH))
    ckrt = jnp.swapaxes(cache_mla_krope, 2, 3)

    mla_out = [[] for _ in range(4)]
    diff_out = [[] for _ in range(4)]
    shp = lambda a: a.reshape(dec_b, dec_t, a.shape[-1])
    for i in range(depth):
        j = i // 2
        w_ffn_in = ffn_w_in[i].astype(BF16)
        w_ffn_out = ffn_w_out[i].astype(BF16)
        g_ffn = row(g_ffn_norm[i])
        if i % 2 == 0:
            p = _mla_params(g_mix_norm[i], mla_w_in[j], mla_g_q_a[j], mla_g_kv_a[j], mla_w_q_b[j], mla_w_kv_b[j],
                            mla_g_qn[j], mla_g_qr[j], mla_g_kn[j], mla_g_kr[j])
            w_o = mla_w_o[j].astype(BF16)
            cgq = (cos_p * p["g_qr"]).T * LOG2E
            sgq = (sin_p * p["g_qr_rot"]).T * LOG2E
            c_p, r_p, qt, k, vt = _mla_proj(xp, cos_p, sin_p, p, True, cgq, sgq)
            op = _mla_prompt_attn(qt, k, vt, batch, seq, t_mla)
            c_s, r_s, qa, qr = _mla_proj(xs, cos_s, sin_s, p, False)
            os_ = _mla_sample_attn(shp(qa), shp(qr), cache_mla_ckv, ckrt, j, shp(c_s), shp(r_s), page_table, p)
            os_ = os_.reshape(dec_b * dec_t, -1)
            for lst, a in zip(mla_out, (c_p.reshape(batch, seq // PAGE_SIZE, PAGE_SIZE, KV_LORA),
                                        r_p.reshape(batch, seq // PAGE_SIZE, PAGE_SIZE, ROPE),
                                        shp(c_s), shp(r_s))):
                lst.append(a)
        else:
            lam_init = 0.8 - 0.6 * math.exp(-0.3 * i)
            lam = (jnp.exp(jnp.sum(diff_lq1[j].astype(F32) * diff_lk1[j].astype(F32)))
                   - jnp.exp(jnp.sum(diff_lq2[j].astype(F32) * diff_lk2[j].astype(F32))) + lam_init).reshape(1)
            gsub = row(diff_g_sub[j]) * (1.0 - lam_init)
            w = diff_w_qkv[j]
            p = {"gmix": row(g_mix_norm[i]), "w_qkv": w.astype(BF16), "w_kv": w[:, dm:].astype(BF16),
                 "w_qv_t": jnp.concatenate([w[:, :dm], w[:, dm + dk:]], axis=1).T.astype(BF16),
                 "g_q": jnp.tile(row(diff_g_qn[j]), (1, LANES // DH)) * DIFF_SCALE,
                 "g_q_col": (diff_g_qn[j].astype(F32) * (DIFF_SCALE * LOG2E)).reshape(-1, 1),
                 "g_k": jnp.tile(row(diff_g_kn[j]), (1, LANES // DH)), "ones64": ones64}
            w_o = diff_w_o[j].astype(BF16)
            k_p, v_p, kb, qt, vt = _diff_proj(xp, p, True, t_diff)
            op = _diff_prompt_attn(qt, kb, vt, bias_t, lam, gsub.reshape(-1, 1), batch, seq, t_diff)
            q_s, k_s, v_s = _diff_proj(xs, p, False)
            os_ = _diff_sample_attn(shp(q_s), ckdk, ckdv, j, shp(k_s), shp(v_s), page_table, bias_tail, lam, gsub)
            os_ = os_.reshape(dec_b * dec_t, -1)
            pg = lambda a: a.reshape(batch, seq // PAGE_SIZE, PAGE_SIZE, DIFF_KV_HEADS, 2 * DH)
            sm = lambda a: a.reshape(dec_b, dec_t, DIFF_KV_HEADS, 2 * DH)
            for lst, a in zip(diff_out, (pg(k_p), pg(v_p), sm(k_s), sm(v_s))):
                lst.append(a)
        xp = _post(xp, op, w_o, g_ffn, w_ffn_in, w_ffn_out)
        xs = _post(xs, os_, w_o, g_ffn, w_ffn_in, w_ffn_out)
    return (xp.reshape(batch, seq, dm), xs.reshape(dec_b, dec_t, dm),
            *(jnp.stack(l) for l in mla_out), *(jnp.stack(l) for l in diff_out))
```

```python
import functools
import math

import jax
import jax.numpy as jnp
from jax import lax
from jax.experimental import pallas as pl
from jax.experimental.pallas import tpu as pltpu

F32 = jnp.float32
BF16 = jnp.bfloat16

EPS = 1e-6
PAGE_SIZE = 128
MLA_HEADS = 8
Q_LORA = 384
KV_LORA = 256
NOPE = 128
ROPE = 64
V_DIM = 128
ROPE_THETA = 10000.0
MLA_SCALE = (NOPE + ROPE) ** -0.5
DH = 64
DIFF_KV_HEADS = 2
DIFF_SCALE = DH ** -0.5
N_BUCKETS = 32
MAX_DISTANCE = 128

LANES = 128
VMEM_LIMIT = 56 * 1024 * 1024
PROJ_ROWS = 512
DIFF_TILE = 256
V_ROWS = V_DIM + 16
MLA_PAGES_PER_STEP = 64
DIFF_PAGES_PER_STEP = 64
NORM_CHUNK = 512
LOG2E = math.log2(math.e)

NT_DIMS = (((1,), (1,)), ((), ()))


def _cparams(sem):
    return pltpu.CompilerParams(dimension_semantics=sem, vmem_limit_bytes=VMEM_LIMIT)


def _rms(x, g):
    return x * lax.rsqrt(jnp.mean(x * x, axis=-1, keepdims=True) + EPS) * g


def _dot(a, b):
    return jnp.dot(a, b, preferred_element_type=F32)


def _dot_nt(a, b):
    return lax.dot_general(a, b, NT_DIMS, preferred_element_type=F32)


def _pad_rows(a, rows):
    return jnp.concatenate([a, jnp.zeros((rows - a.shape[0], a.shape[1]), a.dtype)], axis=0)


def _full(shape):
    return pl.BlockSpec(shape, lambda *_: (0,) * len(shape))


def _row_tile(n, cap):
    t = min(cap, n)
    assert n % t == 0 and t % 8 == 0
    return t


def _mla_proj_kernel(x_ref, cos_ref, sin_ref, gmix_ref, win_ref, gqa_ref, gkva_ref, gkr_ref, gkrr_ref,
                     wqb_ref, gqn_ref, gkn_ref, wk_ref, *rest, prompt):
    if prompt:
        cgq_ref, sgq_ref, wvt_ref, ckv_ref, kr_ref, qt_ref, k_ref, vt_ref = rest
    else:
        gqr_ref, gqrr_ref, ckv_ref, kr_ref, qa_ref, qr_ref = rest
    cos2, sin2 = cos_ref[...], sin_ref[...]
    h = _rms(x_ref[...], gmix_ref[...]).astype(BF16)
    d = _dot(h, win_ref[...])
    cq = _rms(d[:, :Q_LORA], gqa_ref[...]).astype(BF16)
    ckv = _rms(d[:, Q_LORA:Q_LORA + KV_LORA], gkva_ref[...])
    ckv_ref[...] = ckv
    zr = d[:, Q_LORA + KV_LORA:Q_LORA + KV_LORA + LANES]
    zrr = d[:, Q_LORA + KV_LORA + LANES:]
    inv = lax.rsqrt(jnp.sum(zr * zr, axis=-1, keepdims=True) * (1.0 / ROPE) + EPS)
    kr = (zr * gkr_ref[...] * cos2 + zrr * gkrr_ref[...] * sin2) * inv
    kr_ref[...] = kr[:, :ROPE]
    hw = MLA_HEADS * LANES
    cb = ckv.astype(BF16)

    if prompt:
        qall = _dot_nt(wqb_ref[...], cq)
        vt_ref[...] = _with_ones_rows(_dot_nt(wvt_ref[...], cb).astype(BF16))
        kn_all = _dot(cb, wk_ref[...])
        krb = kr.astype(BF16)
        cgq, sgq = cgq_ref[...], sgq_ref[...]
        for hd in range(MLA_HEADS):
            c0 = hd * LANES
            zn = qall[c0:c0 + LANES]
            qn = zn * lax.rsqrt(jnp.mean(zn * zn, axis=0, keepdims=True) + EPS) * gqn_ref[...]
            zr = qall[hw + c0:hw + c0 + LANES]
            zrr = qall[2 * hw + c0:2 * hw + c0 + LANES]
            inv = lax.rsqrt(jnp.sum(zr * zr, axis=0, keepdims=True) * (1.0 / ROPE) + EPS)
            qt_ref[2 * c0:2 * c0 + LANES, :] = qn.astype(BF16)
            qt_ref[2 * c0 + LANES:2 * c0 + 2 * LANES, :] = ((zr * cgq + zrr * sgq) * inv).astype(BF16)
            kn = _rms(kn_all[:, c0:c0 + LANES], gkn_ref[...])
            k_ref[:, 2 * c0:2 * c0 + LANES] = kn.astype(BF16)
            k_ref[:, 2 * c0 + LANES:2 * c0 + 2 * LANES] = krb
    else:
        qall = _dot(cq, wqb_ref[...])
        for hd in range(MLA_HEADS):
            c0 = hd * LANES
            qn = _rms(qall[:, c0:c0 + LANES], gqn_ref[...])
            zr = qall[:, hw + c0:hw + c0 + LANES]
            zrr = qall[:, 2 * hw + c0:2 * hw + c0 + LANES]
            inv = lax.rsqrt(jnp.sum(zr * zr, axis=-1, keepdims=True) * (1.0 / ROPE) + EPS)
            qr_ref[:, c0:c0 + LANES] = (zr * gqr_ref[...] * cos2 + zrr * gqrr_ref[...] * sin2) * inv
            qg = (qn * gkn_ref[...]).astype(BF16)
            qa_ref[:, hd * KV_LORA:(hd + 1) * KV_LORA] = _dot(qg, wk_ref[c0:c0 + LANES, :])


def _mla_proj(x, cos2, sin2, p, prompt, cgq=None, sgq=None):
    n, dm = x.shape
    tm = _row_tile(n, PROJ_ROWS)
    nt = cos2.shape[0] // tm
    hw = MLA_HEADS * LANES
    row = lambda w: pl.BlockSpec((tm, w), lambda i: (i, 0))
    tab = pl.BlockSpec((tm, LANES), lambda i: (i % nt, 0))
    common = [p["gmix"], p["w_in"], p["g_qa"], p["g_kva"], p["g_kr"], p["g_kr_rot"]]
    if prompt:
        tab_t = pl.BlockSpec((LANES, tm), lambda i: (0, i % nt))
        tail = [p["w_qb_t"], p["g_qn_col"], p["g_kn"], p["w_k"]]
        ins = [x, cos2, sin2] + common + tail + [cgq, sgq, p["w_v_t"]]
        specs = ([row(dm), tab, tab] + [_full(a.shape) for a in common + tail]
                 + [tab_t, tab_t, _full(p["w_v_t"].shape)])
        feat = lambda w: pl.BlockSpec((None, w, tm), lambda i: (i, 0, 0))
        out_shape = [jax.ShapeDtypeStruct((n, KV_LORA), F32), jax.ShapeDtypeStruct((n, ROPE), F32),
                     jax.ShapeDtypeStruct((n // tm, 2 * hw, tm), BF16), jax.ShapeDtypeStruct((n, 2 * hw), BF16),
                     jax.ShapeDtypeStruct((n // tm, MLA_HEADS * V_ROWS, tm), BF16)]
        out_specs = [row(KV_LORA), row(ROPE), feat(2 * hw), row(2 * hw), feat(MLA_HEADS * V_ROWS)]
    else:
        tail = [p["w_qb"], p["g_qn"], p["g_kn"], p["w_k_t"], p["g_qr"], p["g_qr_rot"]]
        ins = [x, cos2, sin2] + common + tail
        specs = [row(dm), tab, tab] + [_full(a.shape) for a in common + tail]
        out_shape = [jax.ShapeDtypeStruct((n, KV_LORA), F32), jax.ShapeDtypeStruct((n, ROPE), F32),
                     jax.ShapeDtypeStruct((n, MLA_HEADS * KV_LORA), F32), jax.ShapeDtypeStruct((n, hw), F32)]
        out_specs = [row(KV_LORA), row(ROPE), row(MLA_HEADS * KV_LORA), row(hw)]
    return pl.pallas_call(
        functools.partial(_mla_proj_kernel, prompt=prompt),
        grid=(n // tm,), in_specs=specs, out_specs=out_specs, out_shape=out_shape,
        compiler_params=_cparams(("parallel",)),
        name="mla_proj_prompt" if prompt else "mla_proj_sample",
    )(*ins)


def _softmax_step_t(st, vt, m_ref, acc_ref):
    m_old = m_ref[...]
    m_new = jnp.maximum(m_old, jnp.max(st, axis=0, keepdims=True))
    p = jnp.exp2(st - m_new).astype(BF16)
    acc_ref[...] = jnp.exp2(m_old - m_new) * acc_ref[...] + _dot(vt, p)
    m_ref[...] = m_new


def _with_ones_rows(vt_heads):
    n = vt_heads.shape[1]
    ones = jnp.ones((V_ROWS - V_DIM, n), BF16)
    parts = []
    for h0 in range(0, vt_heads.shape[0], V_DIM):
        parts += [vt_heads[h0:h0 + V_DIM], ones]
    return jnp.concatenate(parts, axis=0)


def _init_stats(m_ref, l_ref, acc_ref):
    m_ref[...] = jnp.full(m_ref.shape, -jnp.inf, F32)
    if l_ref is not None:
        l_ref[...] = jnp.zeros(l_ref.shape, F32)
    acc_ref[...] = jnp.zeros(acc_ref.shape, F32)


def _mla_attn_kernel(qt_ref, k_ref, vt_ref, o_ref, sa_ref, sb_ref, m_ref, acc_ref, *, t):
    i = pl.program_id(2)
    _init_stats(m_ref, None, acc_ref)

    def scores(j, s_ref):
        off = pl.multiple_of(j * t, t)
        s_ref[...] = _dot(k_ref[pl.ds(off, t), :], qt_ref[...])

    def update(j, s_ref, diagonal):
        st = s_ref[...]
        if diagonal:
            keys = lax.broadcasted_iota(jnp.int32, (t, t), 0)
            queries = lax.broadcasted_iota(jnp.int32, (t, t), 1)
            st = jnp.where(keys <= queries, st, -jnp.inf)
        _softmax_step_t(st, vt_ref[j], m_ref, acc_ref)

    scores(0, sa_ref)

    def pair(jj, carry):
        a = 2 * jj
        scores(a + 1, sb_ref)
        update(a, sa_ref, False)
        scores(a + 2, sa_ref)
        update(a + 1, sb_ref, False)
        return carry

    lax.fori_loop(0, i // 2, pair, 0)

    @pl.when(i % 2 == 0)
    def _():
        update(i, sa_ref, True)

    @pl.when(i % 2 == 1)
    def _():
        scores(i, sb_ref)
        update(i - 1, sa_ref, False)
        update(i, sb_ref, True)

    o_ref[...] = (acc_ref[0:V_DIM, :] / acc_ref[V_DIM:V_DIM + 1, :]).T.astype(BF16)


def _mla_prompt_attn(qt, k, vt, batch, seq, t):
    nq = seq // t
    kw = 2 * LANES
    return pl.pallas_call(
        functools.partial(_mla_attn_kernel, t=t),
        grid=(batch, MLA_HEADS, nq),
        in_specs=[pl.BlockSpec((None, kw, t), lambda b, h, i: (b * nq + i, h, 0)),
                  pl.BlockSpec((seq, kw), lambda b, h, i: (b, h)),
                  pl.BlockSpec((nq, V_ROWS, t), lambda b, h, i: (b, h, 0))],
        out_specs=pl.BlockSpec((t, V_DIM), lambda b, h, i: (b * nq + i, h)),
        out_shape=jax.ShapeDtypeStruct((batch * seq, MLA_HEADS * V_DIM), BF16),
        scratch_shapes=[pltpu.VMEM((t, t), F32), pltpu.VMEM((t, t), F32),
                        pltpu.VMEM((1, t), F32), pltpu.VMEM((V_ROWS, t), F32)],
        compiler_params=_cparams(("parallel", "parallel", "arbitrary")),
        name="mla_prompt_attn",
    )(qt, k, vt)


def _softmax_step(s, v, m_ref, l_ref, acc_ref):
    m_old = m_ref[...]
    m_new = jnp.maximum(m_old, jnp.max(s, axis=-1, keepdims=True))
    alpha = jnp.exp(m_old - m_new)
    p = jnp.exp(s - m_new)
    l_ref[...] = alpha * l_ref[...] + jnp.sum(p, axis=-1, keepdims=True)
    acc_ref[...] = alpha * acc_ref[...] + _dot(p.astype(BF16), v)
    m_ref[...] = m_new


def _mla_sample_kernel(pt_ref, qa_ref, qr_ref, *rest, pages, chunk):
    c_pages = rest[:pages]
    r_pages = rest[pages:2 * pages]
    (cnew_ref, rnew_ref, wkt_ref, wv_ref, o_ref,
     wq_ref, qrs_ref, c_ref, r_ref, inv_ref, s_ref, m_ref, l_ref, u_ref) = rest[2 * pages:]
    j = pl.program_id(1)
    t_new = cnew_ref.shape[0]
    rows = MLA_HEADS * t_new
    n_tok = pages * PAGE_SIZE
    kw = MLA_HEADS * NOPE

    @pl.when(j == 0)
    def _():
        _init_stats(m_ref, l_ref, u_ref)
        wq_ref[0:kw, :] = wkt_ref[...]
        wq_ref[kw:, :] = jnp.concatenate(
            [qa_ref[:, hd * KV_LORA:(hd + 1) * KV_LORA] for hd in range(MLA_HEADS)], axis=0).astype(BF16)
        qrs_ref[...] = jnp.concatenate(
            [qr_ref[:, hd * LANES:hd * LANES + ROPE] for hd in range(MLA_HEADS)], axis=0).astype(BF16)

    def expand(lo, hi):
        kt = _dot_nt(wq_ref[...], c_ref[lo:hi, :])
        ssq = jnp.sum((kt[0:kw] * kt[0:kw]).reshape(MLA_HEADS, NOPE, hi - lo), axis=1)
        inv_ref[:, lo:hi] = lax.rsqrt(ssq * (1.0 / NOPE) + EPS)
        s_ref[:, lo:hi] = kt[kw:]

    def attend(n, s_rope, mask):
        s = s_ref[:, 0:n].reshape(MLA_HEADS, t_new, n) * inv_ref[:, 0:n][:, None, :]
        s = s.reshape(rows, n) + s_rope
        if mask is not None:
            s = jnp.where(mask, s, -jnp.inf)
        _softmax_step(s, c_ref[0:n, :], m_ref, l_ref, u_ref)

    for pg in range(pages):
        c_ref[pg * PAGE_SIZE:(pg + 1) * PAGE_SIZE, :] = c_pages[pg][...].astype(BF16)
        r_ref[:, pg * PAGE_SIZE:(pg + 1) * PAGE_SIZE] = r_pages[pg][...].astype(BF16)
    for c0 in range(0, n_tok, chunk):
        expand(c0, c0 + chunk)
    attend(n_tok, _dot(qrs_ref[...], r_ref[...]), None)

    @pl.when(j == pl.num_programs(1) - 1)
    def _():
        c_ref[0:LANES, :] = _pad_rows(cnew_ref[...], LANES).astype(BF16)
        r_new = _pad_rows(rnew_ref[...], LANES).astype(BF16)
        tq = lax.broadcasted_iota(jnp.int32, (rows, LANES), 0) % t_new
        tk = lax.broadcasted_iota(jnp.int32, (rows, LANES), 1)
        expand(0, LANES)
        attend(LANES, _dot_nt(qrs_ref[...], r_new), tk <= tq)
        o_all = _dot((u_ref[...] / l_ref[...]).astype(BF16), wv_ref[...])
        for hd in range(MLA_HEADS):
            o_ref[:, hd * V_DIM:(hd + 1) * V_DIM] = o_all[hd * t_new:(hd + 1) * t_new, hd * V_DIM:(hd + 1) * V_DIM]


def _sample_steps(n_pages, cap):
    pages = min(cap, n_pages)
    assert n_pages % pages == 0
    return pages, n_pages // pages


def _mla_sample_attn(qa, qr, cache_c, cache_rt, layer, c_new, r_new, page_table, p):
    b, t_new, _ = qa.shape
    n_pages = page_table.shape[1]
    pages, steps = _sample_steps(n_pages, MLA_PAGES_PER_STEP)
    chunk = min(NORM_CHUNK, pages * PAGE_SIZE)
    rows = MLA_HEADS * t_new
    per_seq = lambda w: pl.BlockSpec((None, t_new, w), lambda i, j, pt: (i, 0, 0))

    def page_spec(r, w, pg):
        return pl.BlockSpec((None, None, r, w), lambda i, j, pt: (layer, pt[i * n_pages + j * pages + pg], 0, 0))

    in_specs = ([per_seq(MLA_HEADS * KV_LORA), per_seq(MLA_HEADS * LANES)]
                + [page_spec(PAGE_SIZE, KV_LORA, pg) for pg in range(pages)]
                + [page_spec(ROPE, PAGE_SIZE, pg) for pg in range(pages)]
                + [per_seq(KV_LORA), per_seq(ROPE),
                   pl.BlockSpec(p["w_k_t"].shape, lambda i, j, pt: (0, 0)),
                   pl.BlockSpec(p["w_v"].shape, lambda i, j, pt: (0, 0))])
    grid_spec = pltpu.PrefetchScalarGridSpec(
        num_scalar_prefetch=1, grid=(b, steps), in_specs=in_specs,
        out_specs=per_seq(MLA_HEADS * V_DIM),
        scratch_shapes=[pltpu.VMEM((MLA_HEADS * NOPE + rows, KV_LORA), BF16), pltpu.VMEM((rows, ROPE), BF16),
                        pltpu.VMEM((pages * PAGE_SIZE, KV_LORA), BF16), pltpu.VMEM((ROPE, pages * PAGE_SIZE), BF16),
                        pltpu.VMEM((MLA_HEADS, pages * PAGE_SIZE), F32), pltpu.VMEM((rows, pages * PAGE_SIZE), F32),
                        pltpu.VMEM((rows, 1), F32), pltpu.VMEM((rows, 1), F32), pltpu.VMEM((rows, KV_LORA), F32)])
    return pl.pallas_call(
        functools.partial(_mla_sample_kernel, pages=pages, chunk=chunk),
        grid_spec=grid_spec,
        out_shape=jax.ShapeDtypeStruct((b, t_new, MLA_HEADS * V_DIM), F32),
        compiler_params=_cparams(("parallel", "arbitrary")),
        name="mla_sample_attn",
    )(page_table.reshape(-1), qa, qr, *([cache_c] * pages), *([cache_rt] * pages), c_new, r_new,
      p["w_k_t"], p["w_v"])


def _norm64_lanes(z, g, ones):
    ms = _dot((z * z).astype(BF16), ones) * (1.0 / DH)
    return z * lax.rsqrt(ms + EPS) * g


def _diff_proj_sample_kernel(x_ref, gmix_ref, w_ref, gq_ref, gk_ref, ones_ref, q_ref, k_ref, v_ref, *, dq):
    h = _rms(x_ref[...], gmix_ref[...]).astype(BF16)
    qkv = _dot(h, w_ref[...])
    dk = k_ref.shape[1]
    for c0 in range(0, dq, LANES):
        q_ref[:, c0:c0 + LANES] = _norm64_lanes(qkv[:, c0:c0 + LANES], gq_ref[...], ones_ref[...])
    for c0 in range(0, dk, LANES):
        k_ref[:, c0:c0 + LANES] = _norm64_lanes(qkv[:, dq + c0:dq + c0 + LANES], gk_ref[...], ones_ref[...])
    v_ref[...] = qkv[:, dq + dk:]


def _diff_proj_prompt_kernel(x_ref, gmix_ref, wkv_ref, wqvt_ref, gqcol_ref, gk_ref, ones_ref,
                             k_ref, v_ref, kb_ref, qt_ref, vt_ref, *, dq, t):
    h = _rms(x_ref[...], gmix_ref[...]).astype(BF16)
    kv = _dot(h, wkv_ref[...])
    dk = k_ref.shape[1]
    for c0 in range(0, dk, LANES):
        k = _norm64_lanes(kv[:, c0:c0 + LANES], gk_ref[...], ones_ref[...])
        k_ref[:, c0:c0 + LANES] = k
        kb_ref[:, c0:c0 + LANES] = k.astype(BF16)
    v_ref[...] = kv[:, dk:]
    qv = _dot_nt(wqvt_ref[...], h)
    tm = qv.shape[1]
    zq = qv[:dq].reshape(dq // DH, DH, tm)
    ms = jnp.mean(zq * zq, axis=1, keepdims=True)
    qn = (zq * lax.rsqrt(ms + EPS) * gqcol_ref[...]).reshape(dq, tm).astype(BF16)
    vb = _with_ones_rows(qv[dq:].astype(BF16))
    for c in range(tm // t):
        qt_ref[c] = qn[:, c * t:(c + 1) * t]
        vt_ref[c] = vb[:, c * t:(c + 1) * t]


def _diff_proj(x, p, prompt, t=None):
    n, dm = x.shape
    tm = _row_tile(n, PROJ_ROWS)
    dq = dm
    dk = DIFF_KV_HEADS * 2 * DH
    row = lambda w: pl.BlockSpec((tm, w), lambda i: (i, 0))
    if prompt:
        assert tm % t == 0
        ins = [x, p["gmix"], p["w_kv"], p["w_qv_t"], p["g_q_col"], p["g_k"], p["ones64"]]
        feat = lambda w: pl.BlockSpec((tm // t, w, t), lambda i: (i, 0, 0))
        return pl.pallas_call(
            functools.partial(_diff_proj_prompt_kernel, dq=dq, t=t),
            grid=(n // tm,),
            in_specs=[row(dm)] + [_full(a.shape) for a in ins[1:]],
            out_specs=[row(dk), row(dk), row(dk), feat(dq), feat(DIFF_KV_HEADS * V_ROWS)],
            out_shape=[jax.ShapeDtypeStruct((n, dk), F32), jax.ShapeDtypeStruct((n, dk), F32),
                       jax.ShapeDtypeStruct((n, dk), BF16), jax.ShapeDtypeStruct((n // t, dq, t), BF16),
                       jax.ShapeDtypeStruct((n // t, DIFF_KV_HEADS * V_ROWS, t), BF16)],
            compiler_params=_cparams(("parallel",)),
            name="diff_proj_prompt",
        )(*ins)
    ins = [x, p["gmix"], p["w_qkv"], p["g_q"], p["g_k"], p["ones64"]]
    return pl.pallas_call(
        functools.partial(_diff_proj_sample_kernel, dq=dq),
        grid=(n // tm,),
        in_specs=[row(dm)] + [_full(a.shape) for a in ins[1:]],
        out_specs=[row(dq), row(dk), row(dk)],
        out_shape=[jax.ShapeDtypeStruct((n, dq), F32), jax.ShapeDtypeStruct((n, dk), F32),
                   jax.ShapeDtypeStruct((n, dk), F32)],
        compiler_params=_cparams(("parallel",)),
        name="diff_proj_sample",
    )(*ins)


def _diff_attn_kernel(lam_ref, qt_ref, k_ref, vt_ref, bias_ref, gsub_ref, o_ref, qs_ref, sa_ref, sb_ref, m_ref,
                      acc_ref, *, t):
    i = pl.program_id(2)
    group = qt_ref.shape[0] // LANES
    _init_stats(m_ref, None, acc_ref)
    feat = lax.broadcasted_iota(jnp.int32, (LANES, t), 0)
    zero = jnp.zeros((LANES, t), BF16)
    for g in range(group):
        qg = qt_ref[g * LANES:(g + 1) * LANES, :]
        qs_ref[:, (2 * g) * t:(2 * g + 1) * t] = jnp.where(feat < DH, qg, zero)
        qs_ref[:, (2 * g + 1) * t:(2 * g + 2) * t] = jnp.where(feat >= DH, qg, zero)

    def scores(j, s_ref):
        off = pl.multiple_of(j * t, t)
        s_ref[...] = _dot(k_ref[pl.ds(off, t), :], qs_ref[...])

    def update(j, s_ref, near):
        st = s_ref[...]
        if near is not None:
            st = st + bias_ref[near]
        _softmax_step_t(st, vt_ref[j], m_ref, acc_ref)

    n_far = jnp.maximum(i - 1, 0)
    scores(0, sa_ref)

    def pair(jj, carry):
        a = 2 * jj
        scores(a + 1, sb_ref)
        update(a, sa_ref, None)
        scores(a + 2, sa_ref)
        update(a + 1, sb_ref, None)
        return carry

    lax.fori_loop(0, n_far // 2, pair, 0)

    @pl.when(i == 0)
    def _():
        update(0, sa_ref, 0)

    @pl.when((i >= 1) & (n_far % 2 == 0))
    def _():
        scores(i, sb_ref)
        update(i - 1, sa_ref, 1)
        update(i, sb_ref, 0)

    @pl.when(n_far % 2 == 1)
    def _():
        scores(i - 1, sb_ref)
        update(i - 2, sa_ref, None)
        scores(i, sa_ref)
        update(i - 1, sb_ref, 1)
        update(i, sa_ref, 0)

    acc = acc_ref[0:V_DIM, :] / acc_ref[V_DIM:V_DIM + 1, :]
    lam = lam_ref[0]
    for g in range(group):
        o = acc[:, (2 * g) * t:(2 * g + 1) * t] - lam * acc[:, (2 * g + 1) * t:(2 * g + 2) * t]
        o = o * lax.rsqrt(jnp.mean(o * o, axis=0, keepdims=True) + EPS) * gsub_ref[...]
        o_ref[:, g * LANES:(g + 1) * LANES] = o.T.astype(BF16)


def _diff_prompt_attn(qt, k, vt, bias_t, lam, gsub_col, batch, seq, t):
    nq = seq // t
    heads = qt.shape[1] // LANES
    group = heads // DIFF_KV_HEADS
    gw = group * LANES
    r = group * 2 * t
    return pl.pallas_call(
        functools.partial(_diff_attn_kernel, t=t),
        grid=(batch, DIFF_KV_HEADS, nq),
        in_specs=[pl.BlockSpec(memory_space=pltpu.SMEM),
                  pl.BlockSpec((None, gw, t), lambda b, n, i: (b * nq + i, n, 0)),
                  pl.BlockSpec((seq, LANES), lambda b, n, i: (b, n)),
                  pl.BlockSpec((nq, V_ROWS, t), lambda b, n, i: (b, n, 0)),
                  pl.BlockSpec((2, None, t, r), lambda b, n, i: (0, n, 0, 0)),
                  pl.BlockSpec((LANES, 1), lambda b, n, i: (0, 0))],
        out_specs=pl.BlockSpec((t, gw), lambda b, n, i: (b * nq + i, n)),
        out_shape=jax.ShapeDtypeStruct((batch * seq, heads * LANES), BF16),
        scratch_shapes=[pltpu.VMEM((LANES, r), BF16), pltpu.VMEM((t, r), F32), pltpu.VMEM((t, r), F32),
                        pltpu.VMEM((1, r), F32), pltpu.VMEM((V_ROWS, r), F32)],
        compiler_params=_cparams(("parallel", "parallel", "arbitrary")),
        name="diff_prompt_attn",
    )(lam, qt, k, vt, bias_t, gsub_col)


def _stack_diff_queries(q, qs_ref, t):
    lane = lax.broadcasted_iota(jnp.int32, (t, LANES), 1)
    zero = jnp.zeros((t, LANES), q.dtype)
    for g in range(q.shape[1] // LANES):
        qg = q[:, g * LANES:(g + 1) * LANES]
        qs_ref[(2 * g) * t:(2 * g + 1) * t, :] = jnp.where(lane < DH, qg, zero)
        qs_ref[(2 * g + 1) * t:(2 * g + 2) * t, :] = jnp.where(lane >= DH, qg, zero)


def _diff_sample_kernel(pt_ref, lam_ref, q_ref, *rest, pages):
    k_pages = rest[:pages]
    v_pages = rest[pages:2 * pages]
    (knew_ref, vnew_ref, btail_ref, gsub_ref, o_ref,
     qs_ref, k_ref, v_ref, m_ref, l_ref, acc_ref) = rest[2 * pages:]
    j = pl.program_id(1)
    last = pl.num_programs(1) - 1
    t_new = q_ref.shape[0]
    gw = q_ref.shape[1] // DIFF_KV_HEADS
    group = gw // LANES
    rows_n = group * 2 * t_new

    @pl.when(j == 0)
    def _():
        _init_stats(m_ref, l_ref, acc_ref)
        for n in range(DIFF_KV_HEADS):
            _stack_diff_queries(q_ref[:, n * gw:(n + 1) * gw], qs_ref.at[n], t_new)

    def attend(n, lo, hi, bias, mask):
        s = _dot_nt(qs_ref[n].astype(BF16), k_ref[n, lo:hi, :])
        if bias is not None:
            s = s + bias
        if mask is not None:
            s = jnp.where(mask, s, -jnp.inf)
        _softmax_step(s, v_ref[n, lo:hi, :], m_ref.at[n], l_ref.at[n], acc_ref.at[n])

    for pg in range(pages):
        for n in range(DIFF_KV_HEADS):
            head_rows = pl.ds(n, PAGE_SIZE, stride=DIFF_KV_HEADS)
            k_ref[n, pg * PAGE_SIZE:(pg + 1) * PAGE_SIZE, :] = k_pages[pg][head_rows, :].astype(BF16)
            v_ref[n, pg * PAGE_SIZE:(pg + 1) * PAGE_SIZE, :] = v_pages[pg][head_rows, :].astype(BF16)
    n_tok = pages * PAGE_SIZE

    @pl.when(j != last)
    def _():
        for n in range(DIFF_KV_HEADS):
            attend(n, 0, n_tok, None, None)

    @pl.when(j == last)
    def _():
        ca = n_tok - PAGE_SIZE
        tq = lax.broadcasted_iota(jnp.int32, (rows_n, LANES), 0) % t_new
        tk = lax.broadcasted_iota(jnp.int32, (rows_n, LANES), 1)
        lam = lam_ref[0]
        for n in range(DIFF_KV_HEADS):
            if ca > 0:
                attend(n, 0, ca, None, None)
            attend(n, ca, n_tok, btail_ref[n, :, :PAGE_SIZE], None)
            k_ref[n, 0:LANES, :] = _pad_rows(knew_ref[:, n * LANES:(n + 1) * LANES], LANES).astype(BF16)
            v_ref[n, 0:LANES, :] = _pad_rows(vnew_ref[:, n * LANES:(n + 1) * LANES], LANES).astype(BF16)
            attend(n, 0, LANES, btail_ref[n, :, PAGE_SIZE:], tk <= tq)
            acc = acc_ref[n] / l_ref[n]
            for g in range(group):
                a = acc[2 * g * t_new:(2 * g + 2) * t_new]
                o = a[:t_new] - lam * a[t_new:]
                hd = n * group + g
                o_ref[:, hd * LANES:(hd + 1) * LANES] = _rms(o, gsub_ref[...])


def _diff_sample_attn(q, cache_k, cache_v, layer, k_new, v_new, page_table, bias_tail, lam, gsub):
    b, t_new, dq = q.shape
    n_pages = page_table.shape[1]
    pages, steps = _sample_steps(n_pages, DIFF_PAGES_PER_STEP)
    kvw = DIFF_KV_HEADS * LANES
    rows_n = 2 * (dq // LANES // DIFF_KV_HEADS) * t_new
    per_seq = lambda w: pl.BlockSpec((None, t_new, w), lambda i, j, pt: (i, 0, 0))

    def page_spec(pg):
        return pl.BlockSpec((None, None, PAGE_SIZE * DIFF_KV_HEADS, LANES),
                            lambda i, j, pt: (layer, pt[i * n_pages + j * pages + pg], 0, 0))

    in_specs = ([pl.BlockSpec(memory_space=pltpu.SMEM), per_seq(dq)]
                + [page_spec(pg) for pg in range(pages)] * 2
                + [per_seq(kvw), per_seq(kvw),
                   pl.BlockSpec(bias_tail.shape, lambda i, j, pt: (0, 0, 0)),
                   pl.BlockSpec((1, LANES), lambda i, j, pt: (0, 0))])
    per_head = lambda r, w, dt: pltpu.VMEM((DIFF_KV_HEADS, r, w), dt)
    grid_spec = pltpu.PrefetchScalarGridSpec(
        num_scalar_prefetch=1, grid=(b, steps), in_specs=in_specs, out_specs=per_seq(dq),
        scratch_shapes=[per_head(rows_n, LANES, F32),
                        per_head(pages * PAGE_SIZE, LANES, BF16), per_head(pages * PAGE_SIZE, LANES, BF16),
                        per_head(rows_n, 1, F32), per_head(rows_n, 1, F32), per_head(rows_n, LANES, F32)])
    return pl.pallas_call(
        functools.partial(_diff_sample_kernel, pages=pages),
        grid_spec=grid_spec,
        out_shape=jax.ShapeDtypeStruct((b, t_new, dq), F32),
        compiler_params=_cparams(("parallel", "arbitrary")),
        name="diff_sample_attn",
    )(page_table.reshape(-1), lam, q, *([cache_k] * pages), *([cache_v] * pages), k_new, v_new, bias_tail, gsub)


def _post_kernel(x_ref, o_ref, wo_ref, g_ref, win_ref, wout_ref, y_ref, *, hidden, th):
    x1 = x_ref[...] + _dot(o_ref[...].astype(BF16), wo_ref[...])
    h = _rms(x1, g_ref[...]).astype(BF16)
    acc = x1
    for c0 in range(0, hidden, th):
        gate = _dot(h, win_ref[:, c0:c0 + th])
        up = _dot(h, win_ref[:, hidden + c0:hidden + c0 + th])
        act = (gate * jax.nn.sigmoid(gate) * up).astype(BF16)
        acc = acc + _dot(act, wout_ref[c0:c0 + th, :])
    y_ref[...] = acc


def _post(x, o, w_o, g, w_in, w_out):
    n, dm = x.shape
    hidden = w_out.shape[0]
    tm = _row_tile(n, PROJ_ROWS)
    th = 256 if hidden % 256 == 0 else LANES
    assert hidden % th == 0
    row = lambda w: pl.BlockSpec((tm, w), lambda i: (i, 0))
    resident = lambda a: pl.BlockSpec(a.shape, lambda i: (0, 0), pipeline_mode=pl.Buffered(1))
    return pl.pallas_call(
        functools.partial(_post_kernel, hidden=hidden, th=th),
        grid=(n // tm,),
        in_specs=[row(dm), row(o.shape[1]), resident(w_o), _full(g.shape), resident(w_in), resident(w_out)],
        out_specs=row(dm),
        out_shape=jax.ShapeDtypeStruct((n, dm), F32),
        compiler_params=_cparams(("parallel",)),
        name="out_proj_swiglu",
    )(x, o, w_o, g, w_in, w_out)


def _rot_half(a):
    half = a.shape[-1] // 2
    return jnp.concatenate([a[..., half:], a[..., :half]], axis=-1)


def _pad_lanes(a):
    return jnp.concatenate([a, jnp.zeros(a.shape[:-1] + (LANES - a.shape[-1],), a.dtype)], axis=-1)


def _rope_tables(pos):
    half = ROPE // 2
    inv_freq = ROPE_THETA ** (-jnp.arange(half, dtype=F32) / half)
    ang = pos.astype(F32)[:, None] * inv_freq
    cos, sin = jnp.cos(ang), jnp.sin(ang)
    return _pad_lanes(jnp.concatenate([cos, cos], -1)), _pad_lanes(jnp.concatenate([-sin, sin], -1))


def _mla_params(gmix, w_in, g_qa, g_kva, w_qb, w_kvb, g_qn, g_qr, g_kn, g_kr):
    row = lambda a: a.astype(F32).reshape(1, -1)
    w_r = w_in[:, Q_LORA + KV_LORA:]
    w_in_x = jnp.concatenate([w_in[:, :Q_LORA + KV_LORA], _pad_lanes(w_r), _pad_lanes(_rot_half(w_r))], axis=1)
    wq = w_qb.reshape(Q_LORA, MLA_HEADS, NOPE + ROPE)
    wq_r = wq[:, :, NOPE:]
    flat = lambda a: a.reshape(Q_LORA, MLA_HEADS * LANES)
    w_qb_x = jnp.concatenate([flat(wq[:, :, :NOPE]), flat(_pad_lanes(wq_r)), flat(_pad_lanes(_rot_half(wq_r)))], axis=1)
    wkv = w_kvb.reshape(KV_LORA, MLA_HEADS, NOPE + V_DIM)
    w_k = wkv[:, :, :NOPE].reshape(KV_LORA, MLA_HEADS * NOPE)
    w_v = wkv[:, :, NOPE:].reshape(KV_LORA, MLA_HEADS * V_DIM)
    return {
        "gmix": row(gmix), "w_in": w_in_x.astype(BF16), "g_qa": row(g_qa), "g_kva": row(g_kva),
        "g_kr": row(_pad_lanes(g_kr)), "g_kr_rot": row(_pad_lanes(_rot_half(g_kr))),
        "w_qb": w_qb_x.astype(BF16), "w_qb_t": w_qb_x.T.astype(BF16),
        "g_qn": row(g_qn) * MLA_SCALE, "g_qn_col": (g_qn.astype(F32) * (MLA_SCALE * LOG2E)).reshape(-1, 1),
        "g_qr": row(_pad_lanes(g_qr)) * MLA_SCALE, "g_qr_rot": row(_pad_lanes(_rot_half(g_qr))) * MLA_SCALE,
        "g_kn": row(g_kn), "w_k": w_k.astype(BF16), "w_k_t": w_k.T.astype(BF16),
        "w_v": w_v.astype(BF16), "w_v_t": w_v.T.astype(BF16),
    }


def _relative_bias(table, n_dist):
    n = jnp.arange(n_dist, dtype=jnp.int32)
    max_exact = N_BUCKETS // 2
    nf = jnp.maximum(n, 1).astype(F32)
    large = max_exact + (jnp.log(nf / max_exact) / math.log(MAX_DISTANCE / max_exact)
                         * (N_BUCKETS - max_exact)).astype(jnp.int32)
    large = jnp.minimum(large, N_BUCKETS - 1)
    bucket = jnp.where(n < max_exact, n, large)
    tab = table.astype(F32)
    return tab[bucket] - tab[N_BUCKETS - 1]


def _bias_tiles_t(rb, t, group):
    heads = rb.shape[1]
    length = 3 * t - 1
    f = jnp.concatenate([jnp.full((t - 1, heads), -jnp.inf, F32), rb[:2 * t]], axis=0).T
    tiles = []
    for d in range(2):
        w = jnp.roll(f, -(d * t + t - 1), axis=1)
        z = jnp.tile(w, (1, t))[:, :t * (length - 1)].reshape(heads, t, length - 1)[:, :, :t]
        tiles.append(z)
    z = jnp.stack(tiles).reshape(2, DIFF_KV_HEADS, group, 1, t, t)
    z = jnp.broadcast_to(z, (2, DIFF_KV_HEADS, group, 2, t, t))
    return jnp.moveaxis(z, 4, 2).reshape(2, DIFF_KV_HEADS, t, group * 2 * t)


def kernel(x_prompt, x_sample, cache_mla_ckv, cache_mla_krope, cache_diff_k, cache_diff_v, page_table, rel_bias, g_mix_norm, g_ffn_norm, mla_w_in, mla_g_q_a, mla_g_kv_a, mla_w_q_b, mla_w_kv_b, mla_g_qn, mla_g_qr, mla_g_kn, mla_g_kr, mla_w_o, diff_w_qkv, diff_g_qn, diff_g_kn, diff_lq1, diff_lk1, diff_lq2, diff_lk2, diff_g_sub, diff_w_o, ffn_w_in, ffn_w_out):
    batch, seq, dm = x_prompt.shape
    dec_b, dec_t, _ = x_sample.shape
    depth = g_mix_norm.shape[0]
    n_pages = page_table.shape[1]
    past = n_pages * PAGE_SIZE
    heads = dm // (2 * DH)
    group = heads // DIFF_KV_HEADS
    dk = DIFF_KV_HEADS * 2 * DH
    assert past >= MAX_DISTANCE

    xp = x_prompt.reshape(batch * seq, dm)
    xs = x_sample.reshape(dec_b * dec_t, dm)
    row = lambda a: a.astype(F32).reshape(1, -1)

    t_mla = _row_tile(seq, PROJ_ROWS)
    cos_p, sin_p = _rope_tables(jnp.arange(seq, dtype=jnp.int32))
    ts = _row_tile(dec_b * dec_t, PROJ_ROWS)
    cos_s, sin_s = (jnp.tile(a, (ts // dec_t, 1)) for a in _rope_tables(past + jnp.arange(dec_t, dtype=jnp.int32)))

    t_diff = _row_tile(seq, DIFF_TILE)
    assert t_diff >= MAX_DISTANCE
    rb = _relative_bias(rel_bias, max(2 * t_diff, PAGE_SIZE + dec_t))
    bias_t = _bias_tiles_t(rb, t_diff, group) * LOG2E
    tq = jnp.arange(dec_t)
    d_page = PAGE_SIZE + tq[:, None] - jnp.arange(PAGE_SIZE)[None, :]
    d_new = jnp.maximum(tq[:, None] - jnp.arange(LANES)[None, :], 0)
    win = jnp.moveaxis(rb[jnp.concatenate([d_page, d_new], axis=1)], -1, 0)
    win = win.reshape(DIFF_KV_HEADS, group, 1, dec_t, 2 * LANES)
    bias_tail = jnp.broadcast_to(win, (DIFF_KV_HEADS, group, 2, dec_t, 2 * LANES))
    bias_tail = bias_tail.reshape(DIFF_KV_HEADS, -1, 2 * LANES)

    ones64 = jnp.kron(jnp.eye(LANES // DH, dtype=F32), jnp.ones((DH, DH), F32)).astype(BF16)
    ckdk = cache_diff_k.reshape(cache_diff_k.shape[:2] + (PAGE_SIZE * DIFF_KV_HEADS, 2 * DH))
    ckdv = cache_diff_v.reshape(cache_diff_v.shape[:2] + (PAGE_SIZE * DIFF_KV_HEADS, 2 * DH))
    ckrt = jnp.swapaxes(cache_mla_krope, 2, 3)

    mla_out = [[] for _ in range(4)]
    diff_out = [[] for _ in range(4)]
    shp = lambda a: a.reshape(dec_b, dec_t, a.shape[-1])
    for i in range(depth):
        j = i // 2
        w_ffn_in = ffn_w_in[i].astype(BF16)
        w_ffn_out = ffn_w_out[i].astype(BF16)
        g_ffn = row(g_ffn_norm[i])
        if i % 2 == 0:
            p = _mla_params(g_mix_norm[i], mla_w_in[j], mla_g_q_a[j], mla_g_kv_a[j], mla_w_q_b[j], mla_w_kv_b[j],
                            mla_g_qn[j], mla_g_qr[j], mla_g_kn[j], mla_g_kr[j])
            w_o = mla_w_o[j].astype(BF16)
            cgq = (cos_p * p["g_qr"]).T * LOG2E
            sgq = (sin_p * p["g_qr_rot"]).T * LOG2E
            c_p, r_p, qt, k, vt = _mla_proj(xp, cos_p, sin_p, p, True, cgq, sgq)
            op = _mla_prompt_attn(qt, k, vt, batch, seq, t_mla)
            c_s, r_s, qa, qr = _mla_proj(xs, cos_s, sin_s, p, False)
            os_ = _mla_sample_attn(shp(qa), shp(qr), cache_mla_ckv, ckrt, j, shp(c_s), shp(r_s), page_table, p)
            os_ = os_.reshape(dec_b * dec_t, -1)
            for lst, a in zip(mla_out, (c_p.reshape(batch, seq // PAGE_SIZE, PAGE_SIZE, KV_LORA),
                                        r_p.reshape(batch, seq // PAGE_SIZE, PAGE_SIZE, ROPE),
                                        shp(c_s), shp(r_s))):
                lst.append(a)
        else:
            lam_init = 0.8 - 0.6 * math.exp(-0.3 * i)
            lam = (jnp.exp(jnp.sum(diff_lq1[j].astype(F32) * diff_lk1[j].astype(F32)))
                   - jnp.exp(jnp.sum(diff_lq2[j].astype(F32) * diff_lk2[j].astype(F32))) + lam_init).reshape(1)
            gsub = row(diff_g_sub[j]) * (1.0 - lam_init)
            w = diff_w_qkv[j]
            p = {"gmix": row(g_mix_norm[i]), "w_qkv": w.astype(BF16), "w_kv": w[:, dm:].astype(BF16),
                 "w_qv_t": jnp.concatenate([w[:, :dm], w[:, dm + dk:]], axis=1).T.astype(BF16),
                 "g_q": jnp.tile(row(diff_g_qn[j]), (1, LANES // DH)) * DIFF_SCALE,
                 "g_q_col": (diff_g_qn[j].astype(F32) * (DIFF_SCALE * LOG2E)).reshape(-1, 1),
                 "g_k": jnp.tile(row(diff_g_kn[j]), (1, LANES // DH)), "ones64": ones64}
            w_o = diff_w_o[j].astype(BF16)
            k_p, v_p, kb, qt, vt = _diff_proj(xp, p, True, t_diff)
            op = _diff_prompt_attn(qt, kb, vt, bias_t, lam, gsub.reshape(-1, 1), batch, seq, t_diff)
            q_s, k_s, v_s = _diff_proj(xs, p, False)
            os_ = _diff_sample_attn(shp(q_s), ckdk, ckdv, j, shp(k_s), shp(v_s), page_table, bias_tail, lam, gsub)
            os_ = os_.reshape(dec_b * dec_t, -1)
            pg = lambda a: a.reshape(batch, seq // PAGE_SIZE, PAGE_SIZE, DIFF_KV_HEADS, 2 * DH)
            sm = lambda a: a.reshape(dec_b, dec_t, DIFF_KV_HEADS, 2 * DH)
            for lst, a in zip(diff_out, (pg(k_p), pg(v_p), sm(k_s), sm(v_s))):
                lst.append(a)
        xp = _post(xp, op, w_o, g_ffn, w_ffn_in, w_ffn_out)
        xs = _post(xs, os_, w_o, g_ffn, w_ffn_in, w_ffn_out)
    return (xp.reshape(batch, seq, dm), xs.reshape(dec_b, dec_t, dm),
            *(jnp.stack(l) for l in mla_out), *(jnp.stack(l) for l in diff_out))
```
